```python
import jax, jax.numpy as jnp
from jax import lax
import numpy as np

D_MODEL = 1024
BATCH = 4
SEQ = 8192
DEPTH = 2
DEC_BATCH = 16
DEC_SEQ = 4096
PAST_LEN = 128

N_MEM = 256
MLA_HEADS = 8
QK_NOPE = 128
QK_ROPE = 64
V_HEAD = 128
Q_LORA = 384
KV_LORA = 256
ROPE_THETA = 10000.0
FOURIER_GROUPS = 4
FOURIER_GROUP_DIM = 128
D_FOURIER = FOURIER_GROUPS * FOURIER_GROUP_DIM
MEM_HEADS = 4
MEM_HEAD_DIM = 128
D_MEM = MEM_HEADS * MEM_HEAD_DIM
N_BRANCH = 3
D_FF = -(-(8 * D_MODEL) // (3 * 256)) * 256
D_IN = Q_LORA + KV_LORA + QK_ROPE + D_FOURIER + D_MEM + N_BRANCH * D_MODEL
Q_BLOCK = 128
EPS = 1e-6

kernel_name = "hybrid_mla_fourier_memory_encoder"


def _rms_norm(x, g):
    xf = x.astype(jnp.float32)
    y = xf * lax.rsqrt(jnp.mean(xf * xf, axis=-1, keepdims=True) + EPS)
    return y.astype(x.dtype) * g


def _rope_tables(seq_len, dtype):
    inv_freq = ROPE_THETA ** (-(jnp.arange(0, QK_ROPE, 2, dtype=jnp.float32) / QK_ROPE))
    ang = jnp.arange(seq_len, dtype=jnp.float32)[:, None] * inv_freq[None, :]
    return jnp.cos(ang).astype(dtype), jnp.sin(ang).astype(dtype)


def _apply_rope(x, cos, sin):
    half = QK_ROPE // 2
    x1, x2 = x[..., :half], x[..., half:]
    return jnp.concatenate([x1 * cos - x2 * sin, x1 * sin + x2 * cos], axis=-1)


def _mla_attention(q_nope, q_pe, k_nope, k_pe, v):
    B, S, H, _ = q_nope.shape
    nb = S // Q_BLOCK
    scale = (QK_NOPE + QK_ROPE) ** -0.5
    qn = q_nope.reshape(B, nb, Q_BLOCK, H, QK_NOPE).transpose(1, 0, 2, 3, 4)
    qp = q_pe.reshape(B, nb, Q_BLOCK, H, QK_ROPE).transpose(1, 0, 2, 3, 4)

    def block(args):
        qn_b, qp_b = args
        s = (jnp.einsum('bqhd,bkhd->bhqk', qn_b, k_nope)
             + jnp.einsum('bqhr,bkr->bhqk', qp_b, k_pe))
        p = jax.nn.softmax(s.astype(jnp.float32) * scale, axis=-1).astype(v.dtype)
        return jnp.einsum('bhqk,bkhd->bqhd', p, v)

    o = lax.map(block, (qn, qp))
    return o.transpose(1, 0, 2, 3, 4).reshape(B, S, H * V_HEAD)


def _layer(x, mem, attn_norm, w_in, q_norm, w_uq, kv_norm, w_ukv, w_o_mla, w_fourier,
           mem_norm, w_mem_kv, w_o_mem, w_out, ffn_norm, w_gate_up, w_down):
    B, S, _ = x.shape
    dt = x.dtype
    h = _rms_norm(x, attn_norm)
    proj = h @ w_in
    o1 = Q_LORA
    o2 = o1 + KV_LORA
    o3 = o2 + QK_ROPE
    o4 = o3 + D_FOURIER
    o5 = o4 + D_MEM
    c_q, c_kv, k_pe, f_in, q_mem, gates = jnp.split(proj, [o1, o2, o3, o4, o5], axis=-1)

    cos, sin = _rope_tables(S, dt)
    q = (_rms_norm(c_q, q_norm) @ w_uq).reshape(B, S, MLA_HEADS, QK_NOPE + QK_ROPE)
    q_nope, q_pe = q[..., :QK_NOPE], q[..., QK_NOPE:]
    q_pe = _apply_rope(q_pe, cos[:, None, :], sin[:, None, :])
    k_pe = _apply_rope(k_pe, cos, sin)
    kv = (_rms_norm(c_kv, kv_norm) @ w_ukv).reshape(B, S, MLA_HEADS, QK_NOPE + V_HEAD)
    k_nope, v = kv[..., :QK_NOPE], kv[..., QK_NOPE:]
    a = _mla_attention(q_nope, q_pe, k_nope, k_pe, v) @ w_o_mla

    f = f_in.reshape(B, S, FOURIER_GROUPS, FOURIER_GROUP_DIM).astype(jnp.float32)
    f = jnp.fft.fft2(f, axes=(1, 3), norm='ortho').real.astype(dt)
    f = f.reshape(B, S, D_FOURIER) @ w_fourier

    mkv = _rms_norm(mem, mem_norm) @ w_mem_kv
    mk = mkv[..., :D_MEM].reshape(B, N_MEM, MEM_HEADS, MEM_HEAD_DIM)
    mv = mkv[..., D_MEM:].reshape(B, N_MEM, MEM_HEADS, MEM_HEAD_DIM)
    qm = q_mem.reshape(B, S, MEM_HEADS, MEM_HEAD_DIM)
    sm = jnp.einsum('bqhd,bkhd->bhqk', qm, mk).astype(jnp.float32) * (MEM_HEAD_DIM ** -0.5)
    pm = jax.nn.softmax(sm, axis=-1).astype(dt)
    c = jnp.einsum('bhqk,bkhd->bqhd', pm, mv).reshape(B, S, D_MEM) @ w_o_mem

    g = jax.nn.sigmoid(gates.astype(jnp.float32)).astype(dt).reshape(B, S, N_BRANCH, D_MODEL)
    merged = g[:, :, 0, :] * a + g[:, :, 1, :] * f + g[:, :, 2, :] * c
    x = x + merged @ w_out

    hf = _rms_norm(x, ffn_norm)
    gu = hf @ w_gate_up
    x = x + (jax.nn.silu(gu[..., :D_FF]) * gu[..., D_FF:]) @ w_down
    return x


def setup_inputs(seed: int = 0) -> dict:
    key = jax.random.key(seed)
    ks = jax.random.split(key, 24)

    def w(k, shape, fan_in):
        return jax.random.normal(k, shape, jnp.float32) * (fan_in ** -0.5)

    def gain(k, shape):
        return 1.0 + 0.01 * jax.random.normal(k, shape, jnp.float32)

    return {
        'x_prompt': jax.random.normal(ks[0], (BATCH, SEQ, D_MODEL), jnp.float32),
        'x_sample': jax.random.normal(ks[1], (DEC_BATCH, DEC_SEQ, D_MODEL), jnp.float32),
        'mem_prompt': jax.random.normal(ks[2], (BATCH, N_MEM, D_MODEL), jnp.float32),
        'mem_sample': jax.random.normal(ks[3], (DEC_BATCH, N_MEM, D_MODEL), jnp.float32),
        'attn_norm': gain(ks[4], (DEPTH, D_MODEL)),
        'w_in': w(ks[5], (DEPTH, D_MODEL, D_IN), D_MODEL),
        'q_norm': gain(ks[6], (DEPTH, Q_LORA)),
        'w_uq': w(ks[7], (DEPTH, Q_LORA, MLA_HEADS * (QK_NOPE + QK_ROPE)), Q_LORA),
        'kv_norm': gain(ks[8], (DEPTH, KV_LORA)),
        'w_ukv': w(ks[9], (DEPTH, KV_LORA, MLA_HEADS * (QK_NOPE + V_HEAD)), KV_LORA),
        'w_o_mla': w(ks[10], (DEPTH, MLA_HEADS * V_HEAD, D_MODEL), MLA_HEADS * V_HEAD),
        'w_fourier': w(ks[11], (DEPTH, D_FOURIER, D_MODEL), D_FOURIER),
        'mem_norm': gain(ks[12], (DEPTH, D_MODEL)),
        'w_mem_kv': w(ks[13], (DEPTH, D_MODEL, 2 * D_MEM), D_MODEL),
        'w_o_mem': w(ks[14], (DEPTH, D_MEM, D_MODEL), D_MEM),
        'w_out': w(ks[15], (DEPTH, D_MODEL, D_MODEL), D_MODEL),
        'ffn_norm': gain(ks[16], (DEPTH, D_MODEL)),
        'w_gate_up': w(ks[17], (DEPTH, D_MODEL, 2 * D_FF), D_MODEL),
        'w_down': w(ks[18], (DEPTH, D_FF, D_MODEL), D_FF),
        'final_norm': gain(ks[19], (D_MODEL,)),
    }


def reference(x_prompt, x_sample, mem_prompt, mem_sample, attn_norm, w_in, q_norm, w_uq,
              kv_norm, w_ukv, w_o_mla, w_fourier, mem_norm, w_mem_kv, w_o_mem, w_out,
              ffn_norm, w_gate_up, w_down, final_norm):
    yp = x_prompt
    ys = x_sample
    for l in range(DEPTH):
        params = (attn_norm[l], w_in[l], q_norm[l], w_uq[l], kv_norm[l], w_ukv[l],
                  w_o_mla[l], w_fourier[l], mem_norm[l], w_mem_kv[l], w_o_mem[l],
                  w_out[l], ffn_norm[l], w_gate_up[l], w_down[l])
        yp = _layer(yp, mem_prompt, *params)
        ys = _layer(ys, mem_sample, *params)
    y_prompt = _rms_norm(yp, final_norm)
    y_sample = _rms_norm(ys, final_norm)
    return (y_prompt, y_sample)
```

```python
import functools

import jax
import jax.numpy as jnp
import numpy as np
from jax import lax
from jax.experimental import pallas as pl
from jax.experimental.pallas import tpu as pltpu

MLA_HEADS = 8
QK_NOPE = 128
QK_ROPE = 64
V_HEAD = 128
Q_LORA = 384
KV_LORA = 256
ROPE_THETA = 10000.0
FOURIER_GROUPS = 4
FOURIER_GROUP_DIM = 128
D_FOURIER = FOURIER_GROUPS * FOURIER_GROUP_DIM
MEM_HEADS = 4
MEM_HEAD_DIM = 128
D_MEM = MEM_HEADS * MEM_HEAD_DIM
EPS = 1e-6

LANES = 128
QK_PAD = QK_NOPE + LANES
VMEM_LIMIT = 56 * 1024 * 1024

F32 = jnp.float32
BF16 = jnp.bfloat16


def _rms(x, g):
    return x * lax.rsqrt(jnp.mean(x * x, axis=-1, keepdims=True) + EPS) * g


def _dot(a, b):
    return jnp.dot(a, b, preferred_element_type=F32)


def _dot_nt(a, b):
    return lax.dot_general(a, b, (((1,), (1,)), ((), ())), preferred_element_type=F32)


def _rope(x, c, s):
    return x * c + pltpu.roll(x, LANES // 2, axis=1) * s


def _const_spec(shape):
    nd = len(shape)
    return pl.BlockSpec(shape, lambda *_: (0,) * nd, pipeline_mode=pl.Buffered(1))


def _params(*sem):
    return pltpu.CompilerParams(dimension_semantics=sem, vmem_limit_bytes=VMEM_LIMIT)


def _proj_kernel(x_ref, g_ref, w1_ref, qn_ref, wuq_ref, kvn_ref, wukv_ref, cdft_ref, rc_ref, rs_ref,
                 q_ref, k_ref, v_ref, xc_ref, xs_ref):
    x = x_ref[0]
    h = _rms(x, g_ref[...]).astype(BF16)
    p = _dot(h, w1_ref[...])
    o1 = Q_LORA
    o2 = o1 + KV_LORA
    o3 = o2 + LANES
    rc = rc_ref[...]
    rs = rs_ref[...]
    cq = _rms(p[:, :o1], qn_ref[...]).astype(BF16)
    q = _dot(cq, wuq_ref[...])
    ckv = _rms(p[:, o1:o2], kvn_ref[...]).astype(BF16)
    kv = _dot(ckv, wukv_ref[...])
    kpe = _rope(p[:, o2:o3], rc, rs).astype(BF16)
    for hd in range(MLA_HEADS):
        b0 = hd * QK_PAD
        q_ref[0, hd, :, :QK_NOPE] = q[:, b0:b0 + QK_NOPE].astype(BF16)
        q_ref[0, hd, :, QK_NOPE:] = _rope(q[:, b0 + QK_NOPE:b0 + QK_PAD], rc, rs).astype(BF16)
        c0 = hd * (QK_NOPE + V_HEAD)
        k_ref[0, hd, :, :QK_NOPE] = kv[:, c0:c0 + QK_NOPE].astype(BF16)
        k_ref[0, hd, :, QK_NOPE:] = kpe
        v_ref[0, hd] = kv[:, c0 + QK_NOPE:c0 + QK_NOPE + V_HEAD].astype(BF16)
    f = p[:, o3:].astype(BF16)
    cdft = cdft_ref[...]
    for g in range(FOURIER_GROUPS):
        lo = g * FOURIER_GROUP_DIM
        r = _dot(f[:, lo:lo + FOURIER_GROUP_DIM], cdft)
        xc_ref[:, lo:lo + FOURIER_GROUP_DIM] = r[:, :FOURIER_GROUP_DIM].astype(BF16)
        xs_ref[:, lo:lo + FOURIER_GROUP_DIM] = r[:, FOURIER_GROUP_DIM:].astype(BF16)


def _proj(x, g, w1, qn, wuq, kvn, wukv, cdft, rc, rs, tm):
    B, S, D = x.shape
    H = MLA_HEADS
    nw1 = w1.shape[1]
    grid = (B, S // tm)
    return pl.pallas_call(
        _proj_kernel,
        grid=grid,
        in_specs=[
            pl.BlockSpec((1, tm, D), lambda b, i: (b, i, 0)),
            _const_spec((1, D)),
            _const_spec((D, nw1)),
            _const_spec((1, Q_LORA)),
            _const_spec((Q_LORA, H * QK_PAD)),
            _const_spec((1, KV_LORA)),
            _const_spec((KV_LORA, H * (QK_NOPE + V_HEAD))),
            _const_spec((FOURIER_GROUP_DIM, 2 * FOURIER_GROUP_DIM)),
            pl.BlockSpec((tm, LANES), lambda b, i: (i, 0)),
            pl.BlockSpec((tm, LANES), lambda b, i: (i, 0)),
        ],
        out_specs=[
            pl.BlockSpec((1, H, tm, QK_PAD), lambda b, i: (b, 0, i, 0)),
            pl.BlockSpec((1, H, tm, QK_PAD), lambda b, i: (b, 0, i, 0)),
            pl.BlockSpec((1, H, tm, V_HEAD), lambda b, i: (b, 0, i, 0)),
            pl.BlockSpec((tm, D_FOURIER), lambda b, i: (i, b)),
            pl.BlockSpec((tm, D_FOURIER), lambda b, i: (i, b)),
        ],
        out_shape=[
            jax.ShapeDtypeStruct((B, H, S, QK_PAD), BF16),
            jax.ShapeDtypeStruct((B, H, S, QK_PAD), BF16),
            jax.ShapeDtypeStruct((B, H, S, V_HEAD), BF16),
            jax.ShapeDtypeStruct((S, B * D_FOURIER), BF16),
            jax.ShapeDtypeStruct((S, B * D_FOURIER), BF16),
        ],
        compiler_params=_params("parallel", "parallel"),
        name="proj",
    )(x, g, w1, qn, wuq, kvn, wukv, cdft, rc, rs)


def _attn_kernel(q_ref, k_ref, v_ref, o_ref, m_sc, l_sc, acc_sc, *, tk, scale):
    S = k_ref.shape[2]
    q = q_ref[0, 0]
    m_sc[...] = jnp.full(m_sc.shape, -jnp.inf, F32)
    l_sc[...] = jnp.zeros(l_sc.shape, F32)
    acc_sc[...] = jnp.zeros(acc_sc.shape, F32)

    def body(j, carry):
        ks = pl.multiple_of(j * tk, tk)
        k = k_ref[0, 0, pl.ds(ks, tk), :]
        v = v_ref[0, 0, pl.ds(ks, tk), :]
        s = _dot_nt(q, k) * scale
        m_prev = m_sc[...]
        m_new = jnp.maximum(m_prev, jnp.max(s, axis=-1, keepdims=True))
        alpha = jnp.exp(m_prev - m_new)
        p = jnp.exp(s - m_new)
        l_sc[...] = alpha * l_sc[...] + jnp.sum(p, axis=-1, keepdims=True)
        acc_sc[...] = alpha * acc_sc[...] + _dot(p.astype(BF16), v)
        m_sc[...] = m_new
        return carry

    lax.fori_loop(0, S // tk, body, 0)
    o_ref[0] = (acc_sc[...] / l_sc[...]).astype(BF16)


def _attention(q, k, v, tq, tk):
    B, H, S, _ = q.shape
    scale = (QK_NOPE + QK_ROPE) ** -0.5
    return pl.pallas_call(
        functools.partial(_attn_kernel, tk=tk, scale=scale),
        grid=(B, H, S // tq),
        in_specs=[
            pl.BlockSpec((1, 1, tq, QK_PAD), lambda b, h, i: (b, h, i, 0)),
            pl.BlockSpec((1, 1, S, QK_PAD), lambda b, h, i: (b, h, 0, 0)),
            pl.BlockSpec((1, 1, S, V_HEAD), lambda b, h, i: (b, h, 0, 0)),
        ],
        out_specs=pl.BlockSpec((1, tq, V_HEAD), lambda b, h, i: (b, i, h)),
        out_shape=jax.ShapeDtypeStruct((B, S, H * V_HEAD), BF16),
        scratch_shapes=[
            pltpu.VMEM((tq, 1), F32),
            pltpu.VMEM((tq, 1), F32),
            pltpu.VMEM((tq, V_HEAD), F32),
        ],
        compiler_params=_params("parallel", "parallel", "arbitrary"),
        name="attn",
    )(q, k, v)


def _seqdft_kernel(ct_ref, st_ref, xc_ref, xs_ref, y_ref, acc_sc):
    kk = pl.program_id(2)

    @pl.when(kk == 0)
    def _():
        acc_sc[...] = jnp.zeros(acc_sc.shape, F32)

    acc_sc[...] += _dot(ct_ref[...], xc_ref[...]) + _dot(st_ref[...], xs_ref[...])

    @pl.when(kk == pl.num_programs(2) - 1)
    def _():
        y_ref[...] = acc_sc[...].astype(BF16)


def _seqdft(ct, st, xc, xs, tm, tn, tk):
    S, N = xc.shape
    return pl.pallas_call(
        _seqdft_kernel,
        grid=(S // tm, N // tn, S // tk),
        in_specs=[
            pl.BlockSpec((tm, tk), lambda i, j, k: (i, k)),
            pl.BlockSpec((tm, tk), lambda i, j, k: (i, k)),
            pl.BlockSpec((tk, tn), lambda i, j, k: (k, j)),
            pl.BlockSpec((tk, tn), lambda i, j, k: (k, j)),
        ],
        out_specs=pl.BlockSpec((tm, tn), lambda i, j, k: (i, j)),
        out_shape=jax.ShapeDtypeStruct((S, N), BF16),
        scratch_shapes=[pltpu.VMEM((tm, tn), F32)],
        compiler_params=_params("parallel", "parallel", "arbitrary"),
        name="seqdft",
    )(ct, st, xc, xs)


def _memkv_kernel(m_ref, g_ref, w_ref, mk_ref, mv_ref):
    h = _rms(m_ref[0], g_ref[...]).astype(BF16)
    kv = _dot(h, w_ref[...])
    mk_ref[0] = kv[:, :D_MEM].astype(BF16)
    mv_ref[0] = kv[:, D_MEM:].astype(BF16)


def _memkv(mem, g, w):
    B, M, D = mem.shape
    return pl.pallas_call(
        _memkv_kernel,
        grid=(B,),
        in_specs=[
            pl.BlockSpec((1, M, D), lambda b: (b, 0, 0)),
            _const_spec((1, D)),
            _const_spec((D, 2 * D_MEM)),
        ],
        out_specs=[
            pl.BlockSpec((1, M, D_MEM), lambda b: (b, 0, 0)),
            pl.BlockSpec((1, M, D_MEM), lambda b: (b, 0, 0)),
        ],
        out_shape=[
            jax.ShapeDtypeStruct((B, M, D_MEM), BF16),
            jax.ShapeDtypeStruct((B, M, D_MEM), BF16),
        ],
        compiler_params=_params("parallel"),
        name="memkv",
    )(mem, g, w)


def _merge_kernel(x_ref, g_ref, wg_ref, wqm_ref, att_ref, y_ref, mk_ref, mv_ref,
                  womla_ref, wf_ref, womem_ref, wout_ref, o_ref):
    x = x_ref[0]
    D = x.shape[-1]
    h = _rms(x, g_ref[...]).astype(BF16)
    qm = _dot(h, wqm_ref[...]).astype(BF16)
    mk = mk_ref[0]
    mv = mv_ref[0]
    mscale = MEM_HEAD_DIM ** -0.5
    ctx = []
    for hd in range(MEM_HEADS):
        lo = hd * MEM_HEAD_DIM
        s = _dot_nt(qm[:, lo:lo + MEM_HEAD_DIM], mk[:, lo:lo + MEM_HEAD_DIM]) * mscale
        s = s - jnp.max(s, axis=-1, keepdims=True)
        e = jnp.exp(s)
        pm = (e / jnp.sum(e, axis=-1, keepdims=True)).astype(BF16)
        ctx.append(_dot(pm, mv[:, lo:lo + MEM_HEAD_DIM]).astype(BF16))
    c = _dot(jnp.concatenate(ctx, axis=-1), womem_ref[...])
    a = _dot(att_ref[0], womla_ref[...])
    f = _dot(y_ref[...], wf_ref[...])
    merged = jax.nn.sigmoid(_dot(h, wg_ref[:, :D])) * a
    merged += jax.nn.sigmoid(_dot(h, wg_ref[:, D:2 * D])) * f
    merged += jax.nn.sigmoid(_dot(h, wg_ref[:, 2 * D:])) * c
    o_ref[0] = x + _dot(merged.astype(BF16), wout_ref[...])


def _merge(x, g, wg, wqm, att, y, mk, mv, womla, wf, womem, wout, tm):
    B, S, D = x.shape
    M = mk.shape[1]
    return pl.pallas_call(
        _merge_kernel,
        grid=(B, S // tm),
        in_specs=[
            pl.BlockSpec((1, tm, D), lambda b, i: (b, i, 0)),
            _const_spec((1, D)),
            _const_spec(wg.shape),
            _const_spec(wqm.shape),
            pl.BlockSpec((1, tm, MLA_HEADS * V_HEAD), lambda b, i: (b, i, 0)),
            pl.BlockSpec((tm, D_FOURIER), lambda b, i: (i, b)),
            pl.BlockSpec((1, M, D_MEM), lambda b, i: (b, 0, 0)),
            pl.BlockSpec((1, M, D_MEM), lambda b, i: (b, 0, 0)),
            _const_spec(womla.shape),
            _const_spec(wf.shape),
            _const_spec(womem.shape),
            _const_spec(wout.shape),
        ],
        out_specs=pl.BlockSpec((1, tm, D), lambda b, i: (b, i, 0)),
        out_shape=jax.ShapeDtypeStruct((B, S, D), F32),
        compiler_params=_params("parallel", "parallel"),
        name="merge",
    )(x, g, wg, wqm, att, y, mk, mv, womla, wf, womem, wout)


def _ffn_kernel(x_ref, g_ref, wgu_ref, wd_ref, fg_ref, o_ref, *, n_chunks, final_norm):
    x = x_ref[...]
    d_ff = wd_ref.shape[0]
    cw = d_ff // n_chunks
    h = _rms(x, g_ref[...]).astype(BF16)
    acc = x
    for c in range(n_chunks):
        gate = _dot(h, wgu_ref[:, c * cw:(c + 1) * cw])
        up = _dot(h, wgu_ref[:, d_ff + c * cw:d_ff + (c + 1) * cw])
        act = (gate * jax.nn.sigmoid(gate) * up).astype(BF16)
        acc = acc + _dot(act, wd_ref[c * cw:(c + 1) * cw, :])
    if final_norm:
        acc = _rms(acc, fg_ref[...])
    o_ref[...] = acc


def _ffn(x, g, wgu, wd, fg, tm, final_norm):
    T, D = x.shape
    d_ff = wd.shape[0]
    n_chunks = 2 if (d_ff // 2) % LANES == 0 else 1
    return pl.pallas_call(
        functools.partial(_ffn_kernel, n_chunks=n_chunks, final_norm=final_norm),
        grid=(T // tm,),
        in_specs=[
            pl.BlockSpec((tm, D), lambda i: (i, 0)),
            _const_spec((1, D)),
            _const_spec(wgu.shape),
            _const_spec(wd.shape),
            _const_spec((1, D)),
        ],
        out_specs=pl.BlockSpec((tm, D), lambda i: (i, 0)),
        out_shape=jax.ShapeDtypeStruct((T, D), F32),
        compiler_params=_params("parallel"),
        name="ffn_final" if final_norm else "ffn",
    )(x, g, wgu, wd, fg)


def _rope_tables(S):
    inv_freq = ROPE_THETA ** (-(jnp.arange(0, QK_ROPE, 2, dtype=F32) / QK_ROPE))
    ang = jnp.arange(S, dtype=F32)[:, None] * inv_freq[None, :]
    c, s, z = jnp.cos(ang), jnp.sin(ang), jnp.zeros_like(ang)
    return jnp.concatenate([c, z, c, z], axis=-1), jnp.concatenate([-s, z, s, z], axis=-1)


def _dft_tables(n, norm):
    idx = jnp.arange(n, dtype=jnp.int32)
    r = (idx[:, None] * idx[None, :]) % n
    ang = r.astype(F32) * (2.0 * np.pi / n)
    return jnp.cos(ang) * norm, jnp.sin(ang) * norm


def _pad_rope_cols(w):
    half = QK_ROPE // 2
    z = jnp.zeros(w.shape[:-1] + (half,), w.dtype)
    return jnp.concatenate([w[..., :half], z, w[..., half:], z], axis=-1)


def _layer_weights(l, attn_norm, w_in, q_norm, w_uq, kv_norm, w_ukv, w_o_mla, w_fourier, mem_norm,
                   w_mem_kv, w_o_mem, w_out, ffn_norm, w_gate_up, w_down):
    o1 = Q_LORA
    o2 = o1 + KV_LORA
    o3 = o2 + QK_ROPE
    o4 = o3 + D_FOURIER
    o5 = o4 + D_MEM
    wi = w_in[l]
    w1 = jnp.concatenate([wi[:, :o2], _pad_rope_cols(wi[:, o2:o3]), wi[:, o3:o4]], axis=-1).astype(BF16)
    uq = w_uq[l].reshape(Q_LORA, MLA_HEADS, QK_NOPE + QK_ROPE)
    uq = jnp.concatenate([uq[..., :QK_NOPE], _pad_rope_cols(uq[..., QK_NOPE:])], axis=-1)
    uq = uq.reshape(Q_LORA, MLA_HEADS * QK_PAD).astype(BF16)
    return dict(
        attn_norm=attn_norm[l][None, :], w1=w1, q_norm=q_norm[l][None, :], wuq=uq,
        kv_norm=kv_norm[l][None, :], wukv=w_ukv[l].astype(BF16),
        wqm=wi[:, o4:o5].astype(BF16), wg=wi[:, o5:].astype(BF16),
        womla=w_o_mla[l].astype(BF16), wf=w_fourier[l].astype(BF16),
        mem_norm=mem_norm[l][None, :], wmemkv=w_mem_kv[l].astype(BF16), womem=w_o_mem[l].astype(BF16),
        wout=w_out[l].astype(BF16), ffn_norm=ffn_norm[l][None, :],
        wgu=w_gate_up[l].astype(BF16), wd=w_down[l].astype(BF16),
    )


def _tiles(S):
    return dict(tm=min(512, S), tq=min(512, S), tk=min(1024, S))


def _group_layer(x, mem, w, tabs, final_gain, final_norm):
    B, S, D = x.shape
    t = _tiles(S)
    q, k, v, xc, xs = _proj(x, w["attn_norm"], w["w1"], w["q_norm"], w["wuq"], w["kv_norm"], w["wukv"],
                            tabs["cdft"], tabs["rc"], tabs["rs"], t["tm"])
    att = _attention(q, k, v, t["tq"], t["tk"])
    n = B * D_FOURIER
    y = _seqdft(tabs["ct"], tabs["st"], xc, xs, min(1024, S), min(2048, n), min(512, S))
    mk, mv = _memkv(mem, w["mem_norm"], w["wmemkv"])
    x1 = _merge(x, w["attn_norm"], w["wg"], w["wqm"], att, y, mk, mv,
                w["womla"], w["wf"], w["womem"], w["wout"], t["tm"])
    x2 = _ffn(x1.reshape(B * S, D), w["ffn_norm"], w["wgu"], w["wd"], final_gain, t["tm"], final_norm)
    return x2.reshape(B, S, D)


def _group_tables(S):
    rc, rs = _rope_tables(S)
    cc, sc = _dft_tables(FOURIER_GROUP_DIM, FOURIER_GROUP_DIM ** -0.5)
    ct, st = _dft_tables(S, S ** -0.5)
    return dict(rc=rc, rs=rs, cdft=jnp.concatenate([cc, sc], axis=-1).astype(BF16),
                ct=ct.astype(BF16), st=(-st).astype(BF16))


def kernel(x_prompt, x_sample, mem_prompt, mem_sample, attn_norm, w_in, q_norm, w_uq, kv_norm, w_ukv,
           w_o_mla, w_fourier, mem_norm, w_mem_kv, w_o_mem, w_out, ffn_norm, w_gate_up, w_down, final_norm):
    depth = w_in.shape[0]
    tabs_p = _group_tables(x_prompt.shape[1])
    tabs_s = _group_tables(x_sample.shape[1])
    fg = final_norm[None, :]
    yp, ys = x_prompt, x_sample
    for l in range(depth):
        w = _layer_weights(l, attn_norm, w_in, q_norm, w_uq, kv_norm, w_ukv, w_o_mla, w_fourier, mem_norm,
                           w_mem_kv, w_o_mem, w_out, ffn_norm, w_gate_up, w_down)
        last = l == depth - 1
        yp = _group_layer(yp, mem_prompt, w, tabs_p, fg, last)
        ys = _group_layer(ys, mem_sample, w, tabs_s, fg, last)
    return (yp, ys)
```

```python
import functools

import jax
import jax.numpy as jnp
import numpy as np
from jax import lax
from jax.experimental import pallas as pl
from jax.experimental.pallas import tpu as pltpu

MLA_HEADS = 8
QK_NOPE = 128
QK_ROPE = 64
V_HEAD = 128
Q_LORA = 384
KV_LORA = 256
ROPE_THETA = 10000.0
FOURIER_GROUPS = 4
FOURIER_GROUP_DIM = 128
D_FOURIER = FOURIER_GROUPS * FOURIER_GROUP_DIM
MEM_HEADS = 4
MEM_HEAD_DIM = 128
D_MEM = MEM_HEADS * MEM_HEAD_DIM
EPS = 1e-6

LANES = 128
QK_PAD = QK_NOPE + LANES
VMEM_LIMIT = 56 * 1024 * 1024
Q_PRESCALE = float((QK_NOPE + QK_ROPE) ** -0.5 * np.log2(np.e))

F32 = jnp.float32
BF16 = jnp.bfloat16


def _rms(x, g):
    return x * lax.rsqrt(jnp.mean(x * x, axis=-1, keepdims=True) + EPS) * g


def _dot(a, b):
    return jnp.dot(a, b, preferred_element_type=F32)


def _dot_nt(a, b):
    return lax.dot_general(a, b, (((1,), (1,)), ((), ())), preferred_element_type=F32)


def _rope(x, c, s):
    return x * c + pltpu.roll(x, LANES // 2, axis=1) * s


def _const_spec(shape):
    nd = len(shape)
    return pl.BlockSpec(shape, lambda *_: (0,) * nd, pipeline_mode=pl.Buffered(1))


def _params(*sem):
    return pltpu.CompilerParams(dimension_semantics=sem, vmem_limit_bytes=VMEM_LIMIT)


def _proj_kernel(x_ref, g_ref, w1_ref, qn_ref, wuq_ref, kvn_ref, wukv_ref, cdft_ref, rc_ref, rs_ref,
                 q_ref, k_ref, v_ref, xc_ref, xs_ref):
    x = x_ref[0]
    h = _rms(x, g_ref[...]).astype(BF16)
    p = _dot(h, w1_ref[...])
    o1 = Q_LORA
    o2 = o1 + KV_LORA
    o3 = o2 + LANES
    rc = rc_ref[...]
    rs = rs_ref[...]
    cq = _rms(p[:, :o1], qn_ref[...]).astype(BF16)
    q = _dot(cq, wuq_ref[...]) * Q_PRESCALE
    ckv = _rms(p[:, o1:o2], kvn_ref[...]).astype(BF16)
    kv = _dot(ckv, wukv_ref[...])
    kpe = _rope(p[:, o2:o3], rc, rs).astype(BF16)
    for hd in range(MLA_HEADS):
        b0 = hd * QK_PAD
        q_ref[0, hd, :, :QK_NOPE] = q[:, b0:b0 + QK_NOPE].astype(BF16)
        q_ref[0, hd, :, QK_NOPE:] = _rope(q[:, b0 + QK_NOPE:b0 + QK_PAD], rc, rs).astype(BF16)
        c0 = hd * (QK_NOPE + V_HEAD)
        k_ref[0, hd, :, :QK_NOPE] = kv[:, c0:c0 + QK_NOPE].astype(BF16)
        k_ref[0, hd, :, QK_NOPE:] = kpe
        v_ref[0, hd] = kv[:, c0 + QK_NOPE:c0 + QK_NOPE + V_HEAD].astype(BF16)
    f = p[:, o3:].astype(BF16)
    cdft = cdft_ref[...]
    for g in range(FOURIER_GROUPS):
        lo = g * FOURIER_GROUP_DIM
        r = _dot(f[:, lo:lo + FOURIER_GROUP_DIM], cdft)
        xc_ref[:, lo:lo + FOURIER_GROUP_DIM] = r[:, :FOURIER_GROUP_DIM].astype(BF16)
        xs_ref[:, lo:lo + FOURIER_GROUP_DIM] = r[:, FOURIER_GROUP_DIM:].astype(BF16)


def _proj(x, g, w1, qn, wuq, kvn, wukv, cdft, rc, rs, tm):
    B, S, D = x.shape
    H = MLA_HEADS
    nw1 = w1.shape[1]
    grid = (B, S // tm)
    return pl.pallas_call(
        _proj_kernel,
        grid=grid,
        in_specs=[
            pl.BlockSpec((1, tm, D), lambda b, i: (b, i, 0)),
            _const_spec((1, D)),
            _const_spec((D, nw1)),
            _const_spec((1, Q_LORA)),
            _const_spec((Q_LORA, H * QK_PAD)),
            _const_spec((1, KV_LORA)),
            _const_spec((KV_LORA, H * (QK_NOPE + V_HEAD))),
            _const_spec((FOURIER_GROUP_DIM, 2 * FOURIER_GROUP_DIM)),
            pl.BlockSpec((tm, LANES), lambda b, i: (i, 0)),
            pl.BlockSpec((tm, LANES), lambda b, i: (i, 0)),
        ],
        out_specs=[
            pl.BlockSpec((1, H, tm, QK_PAD), lambda b, i: (b, 0, i, 0)),
            pl.BlockSpec((1, H, tm, QK_PAD), lambda b, i: (b, 0, i, 0)),
            pl.BlockSpec((1, H, tm, V_HEAD), lambda b, i: (b, 0, i, 0)),
            pl.BlockSpec((tm, D_FOURIER), lambda b, i: (i, b)),
            pl.BlockSpec((tm, D_FOURIER), lambda b, i: (i, b)),
        ],
        out_shape=[
            jax.ShapeDtypeStruct((B, H, S, QK_PAD), BF16),
            jax.ShapeDtypeStruct((B, H, S, QK_PAD), BF16),
            jax.ShapeDtypeStruct((B, H, S, V_HEAD), BF16),
            jax.ShapeDtypeStruct((S, B * D_FOURIER), BF16),
            jax.ShapeDtypeStruct((S, B * D_FOURIER), BF16),
        ],
        compiler_params=_params("parallel", "parallel"),
        name="proj",
    )(x, g, w1, qn, wuq, kvn, wukv, cdft, rc, rs)


def _attn_kernel(q_ref, k_ref, v_ref, o_ref, s_sc, p_sc, a_sc, m_sc, l_sc, acc_sc, *, tk, rb):
    S = k_ref.shape[2]
    tq = q_ref.shape[2]
    n = S // tk
    nc = tk // LANES
    q = q_ref[0, 0]

    def scores(j):
        s_sc[j % 2] = _dot_nt(q, k_ref[0, 0, j * tk:(j + 1) * tk, :])

    def softmax(j):
        slot = j % 2
        for r in range(tq // rb):
            rows = slice(r * rb, (r + 1) * rb)
            cols = [s_sc[slot, rows, c * LANES:(c + 1) * LANES] for c in range(nc)]
            mx = cols[0]
            for c in range(1, nc):
                mx = jnp.maximum(mx, cols[c])
            m_cur = jnp.broadcast_to(jnp.max(mx, axis=-1, keepdims=True), (rb, LANES))
            if j == 0:
                m_new = m_cur
            else:
                m_prev = m_sc[rows, :]
                m_new = jnp.maximum(m_prev, m_cur)
                alpha = jnp.exp2(m_prev - m_new)
                a_sc[slot, rows, :] = alpha
            m_sc[rows, :] = m_new
            lsum = None
            for c in range(nc):
                pc = jnp.exp2(cols[c] - m_new)
                lsum = pc if lsum is None else lsum + pc
                p_sc[slot, rows, c * LANES:(c + 1) * LANES] = pc.astype(BF16)
            if j == 0:
                l_sc[rows, :] = lsum
            else:
                l_sc[rows, :] = alpha * l_sc[rows, :] + lsum

    def values(j):
        slot = j % 2
        pv = _dot(p_sc[slot], v_ref[0, 0, j * tk:(j + 1) * tk, :])
        if j == 0:
            acc_sc[...] = pv
        else:
            acc_sc[...] = a_sc[slot] * acc_sc[...] + pv

    scores(0)
    for j in range(n):
        if j + 1 < n:
            scores(j + 1)
        if j >= 1:
            values(j - 1)
        softmax(j)
    values(n - 1)
    l = jnp.sum(l_sc[...], axis=-1, keepdims=True)
    o_ref[0] = (acc_sc[...] / l).astype(BF16)


def _attention(q, k, v, tq, tk):
    B, H, S, _ = q.shape
    return pl.pallas_call(
        functools.partial(_attn_kernel, tk=tk, rb=min(32, tq)),
        grid=(B, H, S // tq),
        in_specs=[
            pl.BlockSpec((1, 1, tq, QK_PAD), lambda b, h, i: (b, h, i, 0)),
            pl.BlockSpec((1, 1, S, QK_PAD), lambda b, h, i: (b, h, 0, 0)),
            pl.BlockSpec((1, 1, S, V_HEAD), lambda b, h, i: (b, h, 0, 0)),
        ],
        out_specs=pl.BlockSpec((1, tq, V_HEAD), lambda b, h, i: (b, i, h)),
        out_shape=jax.ShapeDtypeStruct((B, S, H * V_HEAD), BF16),
        scratch_shapes=[
            pltpu.VMEM((2, tq, tk), F32),
            pltpu.VMEM((2, tq, tk), BF16),
            pltpu.VMEM((2, tq, LANES), F32),
            pltpu.VMEM((tq, LANES), F32),
            pltpu.VMEM((tq, LANES), F32),
            pltpu.VMEM((tq, V_HEAD), F32),
        ],
        compiler_params=_params("parallel", "parallel", "arbitrary"),
        name="attn",
    )(q, k, v)


def _seqdft_kernel(ct_ref, st_ref, xc_ref, xs_ref, y_ref, acc_sc):
    kk = pl.program_id(2)

    @pl.when(kk == 0)
    def _():
        acc_sc[...] = jnp.zeros(acc_sc.shape, F32)

    acc_sc[...] += _dot(ct_ref[...], xc_ref[...]) + _dot(st_ref[...], xs_ref[...])

    @pl.when(kk == pl.num_programs(2) - 1)
    def _():
        y_ref[...] = acc_sc[...].astype(BF16)


def _seqdft(ct, st, xc, xs, tm, tn, tk):
    S, N = xc.shape
    return pl.pallas_call(
        _seqdft_kernel,
        grid=(S // tm, N // tn, S // tk),
        in_specs=[
            pl.BlockSpec((tm, tk), lambda i, j, k: (i, k)),
            pl.BlockSpec((tm, tk), lambda i, j, k: (i, k)),
            pl.BlockSpec((tk, tn), lambda i, j, k: (k, j)),
            pl.BlockSpec((tk, tn), lambda i, j, k: (k, j)),
        ],
        out_specs=pl.BlockSpec((tm, tn), lambda i, j, k: (i, j)),
        out_shape=jax.ShapeDtypeStruct((S, N), BF16),
        scratch_shapes=[pltpu.VMEM((tm, tn), F32)],
        compiler_params=_params("parallel", "parallel", "arbitrary"),
        name="seqdft",
    )(ct, st, xc, xs)


def _memkv_kernel(m_ref, g_ref, w_ref, mk_ref, mv_ref):
    h = _rms(m_ref[0], g_ref[...]).astype(BF16)
    kv = _dot(h, w_ref[...])
    mk_ref[0] = kv[:, :D_MEM].astype(BF16)
    mv_ref[0] = kv[:, D_MEM:].astype(BF16)


def _memkv(mem, g, w):
    B, M, D = mem.shape
    return pl.pallas_call(
        _memkv_kernel,
        grid=(B,),
        in_specs=[
            pl.BlockSpec((1, M, D), lambda b: (b, 0, 0)),
            _const_spec((1, D)),
            _const_spec((D, 2 * D_MEM)),
        ],
        out_specs=[
            pl.BlockSpec((1, M, D_MEM), lambda b: (b, 0, 0)),
            pl.BlockSpec((1, M, D_MEM), lambda b: (b, 0, 0)),
        ],
        out_shape=[
            jax.ShapeDtypeStruct((B, M, D_MEM), BF16),
            jax.ShapeDtypeStruct((B, M, D_MEM), BF16),
        ],
        compiler_params=_params("parallel"),
        name="memkv",
    )(mem, g, w)


def _merge_kernel(x_ref, g_ref, wg_ref, wqm_ref, att_ref, y_ref, mk_ref, mv_ref,
                  womla_ref, wf_ref, womem_ref, wout_ref, o_ref):
    x = x_ref[0]
    D = x.shape[-1]
    h = _rms(x, g_ref[...]).astype(BF16)
    qm = _dot(h, wqm_ref[...]).astype(BF16)
    mk = mk_ref[0]
    mv = mv_ref[0]
    mscale = MEM_HEAD_DIM ** -0.5
    ctx = []
    for hd in range(MEM_HEADS):
        lo = hd * MEM_HEAD_DIM
        s = _dot_nt(qm[:, lo:lo + MEM_HEAD_DIM], mk[:, lo:lo + MEM_HEAD_DIM]) * mscale
        s = s - jnp.max(s, axis=-1, keepdims=True)
        e = jnp.exp(s)
        pm = (e / jnp.sum(e, axis=-1, keepdims=True)).astype(BF16)
        ctx.append(_dot(pm, mv[:, lo:lo + MEM_HEAD_DIM]).astype(BF16))
    c = _dot(jnp.concatenate(ctx, axis=-1), womem_ref[...])
    a = _dot(att_ref[0], womla_ref[...])
    f = _dot(y_ref[...], wf_ref[...])
    merged = jax.nn.sigmoid(_dot(h, wg_ref[:, :D])) * a
    merged += jax.nn.sigmoid(_dot(h, wg_ref[:, D:2 * D])) * f
    merged += jax.nn.sigmoid(_dot(h, wg_ref[:, 2 * D:])) * c
    o_ref[0] = x + _dot(merged.astype(BF16), wout_ref[...])


def _merge(x, g, wg, wqm, att, y, mk, mv, womla, wf, womem, wout, tm):
    B, S, D = x.shape
    M = mk.shape[1]
    return pl.pallas_call(
        _merge_kernel,
        grid=(B, S // tm),
        in_specs=[
            pl.BlockSpec((1, tm, D), lambda b, i: (b, i, 0)),
            _const_spec((1, D)),
            _const_spec(wg.shape),
            _const_spec(wqm.shape),
            pl.BlockSpec((1, tm, MLA_HEADS * V_HEAD), lambda b, i: (b, i, 0)),
            pl.BlockSpec((tm, D_FOURIER), lambda b, i: (i, b)),
            pl.BlockSpec((1, M, D_MEM), lambda b, i: (b, 0, 0)),
            pl.BlockSpec((1, M, D_MEM), lambda b, i: (b, 0, 0)),
            _const_spec(womla.shape),
            _const_spec(wf.shape),
            _const_spec(womem.shape),
            _const_spec(wout.shape),
        ],
        out_specs=pl.BlockSpec((1, tm, D), lambda b, i: (b, i, 0)),
        out_shape=jax.ShapeDtypeStruct((B, S, D), F32),
        compiler_params=_params("parallel", "parallel"),
        name="merge",
    )(x, g, wg, wqm, att, y, mk, mv, womla, wf, womem, wout)


def _ffn_kernel(x_ref, g_ref, wgu_ref, wd_ref, fg_ref, o_ref, *, n_chunks, final_norm):
    x = x_ref[...]
    d_ff = wd_ref.shape[0]
    cw = d_ff // n_chunks
    h = _rms(x, g_ref[...]).astype(BF16)
    acc = x
    for c in range(n_chunks):
        gate = _dot(h, wgu_ref[:, c * cw:(c + 1) * cw])
        up = _dot(h, wgu_ref[:, d_ff + c * cw:d_ff + (c + 1) * cw])
        act = (gate * jax.nn.sigmoid(gate) * up).astype(BF16)
        acc = acc + _dot(act, wd_ref[c * cw:(c + 1) * cw, :])
    if final_norm:
        acc = _rms(acc, fg_ref[...])
    o_ref[...] = acc


def _ffn(x, g, wgu, wd, fg, tm, final_norm):
    T, D = x.shape
    d_ff = wd.shape[0]
    n_chunks = 2 if (d_ff // 2) % LANES == 0 else 1
    return pl.pallas_call(
        functools.partial(_ffn_kernel, n_chunks=n_chunks, final_norm=final_norm),
        grid=(T // tm,),
        in_specs=[
            pl.BlockSpec((tm, D), lambda i: (i, 0)),
            _const_spec((1, D)),
            _const_spec(wgu.shape),
            _const_spec(wd.shape),
            _const_spec((1, D)),
        ],
        out_specs=pl.BlockSpec((tm, D), lambda i: (i, 0)),
        out_shape=jax.ShapeDtypeStruct((T, D), F32),
        compiler_params=_params("parallel"),
        name="ffn_final" if final_norm else "ffn",
    )(x, g, wgu, wd, fg)


def _rope_tables(S):
    inv_freq = ROPE_THETA ** (-(jnp.arange(0, QK_ROPE, 2, dtype=F32) / QK_ROPE))
    ang = jnp.arange(S, dtype=F32)[:, None] * inv_freq[None, :]
    c, s, z = jnp.cos(ang), jnp.sin(ang), jnp.zeros_like(ang)
    return jnp.concatenate([c, z, c, z], axis=-1), jnp.concatenate([-s, z, s, z], axis=-1)


def _dft_tables(n, norm):
    w = min(n, LANES)
    j = jnp.arange(n, dtype=jnp.int32)[:, None]

    def narrow(k):
        ang = ((j * k[None, :]) % n).astype(F32) * (2.0 * np.pi / n)
        return jnp.cos(ang), jnp.sin(ang)

    ch, sh = narrow(jnp.arange(n // w, dtype=jnp.int32) * w)
    cl, sl = narrow(jnp.arange(w, dtype=jnp.int32))
    cl, sl = cl * norm, sl * norm
    c = ch[:, :, None] * cl[:, None, :] - sh[:, :, None] * sl[:, None, :]
    s = sh[:, :, None] * cl[:, None, :] + ch[:, :, None] * sl[:, None, :]
    return c.reshape(n, n), s.reshape(n, n)


def _pad_rope_cols(w):
    half = QK_ROPE // 2
    z = jnp.zeros(w.shape[:-1] + (half,), w.dtype)
    return jnp.concatenate([w[..., :half], z, w[..., half:], z], axis=-1)


def _layer_weights(l, attn_norm, w_in, q_norm, w_uq, kv_norm, w_ukv, w_o_mla, w_fourier, mem_norm,
                   w_mem_kv, w_o_mem, w_out, ffn_norm, w_gate_up, w_down):
    o1 = Q_LORA
    o2 = o1 + KV_LORA
    o3 = o2 + QK_ROPE
    o4 = o3 + D_FOURIER
    o5 = o4 + D_MEM
    wi = w_in[l]
    w1 = jnp.concatenate([wi[:, :o2], _pad_rope_cols(wi[:, o2:o3]), wi[:, o3:o4]], axis=-1).astype(BF16)
    uq = w_uq[l].reshape(Q_LORA, MLA_HEADS, QK_NOPE + QK_ROPE)
    uq = jnp.concatenate([uq[..., :QK_NOPE], _pad_rope_cols(uq[..., QK_NOPE:])], axis=-1)
    uq = uq.reshape(Q_LORA, MLA_HEADS * QK_PAD).astype(BF16)
    return dict(
        attn_norm=attn_norm[l][None, :], w1=w1, q_norm=q_norm[l][None, :], wuq=uq,
        kv_norm=kv_norm[l][None, :], wukv=w_ukv[l].astype(BF16),
        wqm=wi[:, o4:o5].astype(BF16), wg=wi[:, o5:].astype(BF16),
        womla=w_o_mla[l].astype(BF16), wf=w_fourier[l].astype(BF16),
        mem_norm=mem_norm[l][None, :], wmemkv=w_mem_kv[l].astype(BF16), womem=w_o_mem[l].astype(BF16),
        wout=w_out[l].astype(BF16), ffn_norm=ffn_norm[l][None, :],
        wgu=w_gate_up[l].astype(BF16), wd=w_down[l].astype(BF16),
    )


def _tiles(S):
    return dict(tm=min(512, S), tq=min(1024, S), tk=min(1024, S))


def _group_layer(x, mem, w, tabs, final_gain, final_norm):
    B, S, D = x.shape
    t = _tiles(S)
    q, k, v, xc, xs = _proj(x, w["attn_norm"], w["w1"], w["q_norm"], w["wuq"], w["kv_norm"], w["wukv"],
                            tabs["cdft"], tabs["rc"], tabs["rs"], t["tm"])
    att = _attention(q, k, v, t["tq"], t["tk"])
    n = B * D_FOURIER
    y = _seqdft(tabs["ct"], tabs["st"], xc, xs, min(1024, S), min(2048, n), min(512, S))
    mk, mv = _memkv(mem, w["mem_norm"], w["wmemkv"])
    x1 = _merge(x, w["attn_norm"], w["wg"], w["wqm"], att, y, mk, mv,
                w["womla"], w["wf"], w["womem"], w["wout"], t["tm"])
    x2 = _ffn(x1.reshape(B * S, D), w["ffn_norm"], w["wgu"], w["wd"], final_gain, t["tm"], final_norm)
    return x2.reshape(B, S, D)


def _group_tables(S):
    rc, rs = _rope_tables(S)
    cc, sc = _dft_tables(FOURIER_GROUP_DIM, FOURIER_GROUP_DIM ** -0.5)
    ct, st = _dft_tables(S, S ** -0.5)
    return dict(rc=rc, rs=rs, cdft=jnp.concatenate([cc, sc], axis=-1).astype(BF16),
                ct=ct.astype(BF16), st=(-st).astype(BF16))


def kernel(x_prompt, x_sample, mem_prompt, mem_sample, attn_norm, w_in, q_norm, w_uq, kv_norm, w_ukv,
           w_o_mla, w_fourier, mem_norm, w_mem_kv, w_o_mem, w_out, ffn_norm, w_gate_up, w_down, final_norm):
    depth = w_in.shape[0]
    tabs_p = _group_tables(x_prompt.shape[1])
    tabs_s = _group_tables(x_sample.shape[1])
    fg = final_norm[None, :]
    yp, ys = x_prompt, x_sample
    for l in range(depth):
        w = _layer_weights(l, attn_norm, w_in, q_norm, w_uq, kv_norm, w_ukv, w_o_mla, w_fourier, mem_norm,
                           w_mem_kv, w_o_mem, w_out, ffn_norm, w_gate_up, w_down)
        last = l == depth - 1
        yp = _group_layer(yp, mem_prompt, w, tabs_p, fg, last)
        ys = _group_layer(ys, mem_sample, w, tabs_s, fg, last)
    return (yp, ys)
```

```python
import functools

import jax
import jax.numpy as jnp
import numpy as np
from jax import lax
from jax.experimental import pallas as pl
from jax.experimental.pallas import tpu as pltpu

MLA_HEADS = 8
QK_NOPE = 128
QK_ROPE = 64
V_HEAD = 128
Q_LORA = 384
KV_LORA = 256
ROPE_THETA = 10000.0
FOURIER_GROUPS = 4
FOURIER_GROUP_DIM = 128
D_FOURIER = FOURIER_GROUPS * FOURIER_GROUP_DIM
MEM_HEADS = 4
MEM_HEAD_DIM = 128
D_MEM = MEM_HEADS * MEM_HEAD_DIM
EPS = 1e-6

LANES = 128
QK_PAD = QK_NOPE + LANES
VMEM_LIMIT = 56 * 1024 * 1024
DFT_RADIX = 4
SUBLANES = 8
Q_PRESCALE = float((QK_NOPE + QK_ROPE) ** -0.5 * np.log2(np.e))

F32 = jnp.float32
BF16 = jnp.bfloat16


def _rms(x, g):
    return x * lax.rsqrt(jnp.mean(x * x, axis=-1, keepdims=True) + EPS) * g


def _dot(a, b):
    return jnp.dot(a, b, preferred_element_type=F32)


def _dot_nt(a, b):
    return lax.dot_general(a, b, (((1,), (1,)), ((), ())), preferred_element_type=F32)


def _rope(x, c, s):
    return x * c + pltpu.roll(x, LANES // 2, axis=1) * s


def _const_spec(shape):
    nd = len(shape)
    return pl.BlockSpec(shape, lambda *_: (0,) * nd, pipeline_mode=pl.Buffered(1))


def _params(*sem, flags=None):
    return pltpu.CompilerParams(dimension_semantics=sem, vmem_limit_bytes=VMEM_LIMIT, flags=flags)


def _proj_kernel(x_ref, g_ref, w1_ref, qn_ref, wuq_ref, kvn_ref, wukv_ref, rc_ref, rs_ref,
                 q_ref, k_ref, v_ref, xr_ref, f_sc):
    x = x_ref[0]
    tm = x.shape[0]
    h = _rms(x, g_ref[...]).astype(BF16)
    p = _dot(h, w1_ref[...])
    o1 = Q_LORA
    o2 = o1 + KV_LORA
    o3 = o2 + LANES
    rc = rc_ref[...]
    rs = rs_ref[...]
    cq = _rms(p[:, :o1], qn_ref[...]).astype(BF16)
    q = _dot(cq, wuq_ref[...]) * Q_PRESCALE
    ckv = _rms(p[:, o1:o2], kvn_ref[...]).astype(BF16)
    kv = _dot(ckv, wukv_ref[...])
    kpe = _rope(p[:, o2:o3], rc, rs).astype(BF16)
    for hd in range(MLA_HEADS):
        b0 = hd * QK_PAD
        q_ref[0, hd, :, :QK_NOPE] = q[:, b0:b0 + QK_NOPE].astype(BF16)
        q_ref[0, hd, :, QK_NOPE:] = _rope(q[:, b0 + QK_NOPE:b0 + QK_PAD], rc, rs).astype(BF16)
        c0 = hd * (QK_NOPE + V_HEAD)
        k_ref[0, hd, :, :QK_NOPE] = kv[:, c0:c0 + QK_NOPE].astype(BF16)
        k_ref[0, hd, :, QK_NOPE:] = kpe
        v_ref[0, hd, :, :V_HEAD] = kv[:, c0 + QK_NOPE:c0 + QK_NOPE + V_HEAD].astype(BF16)
        v_ref[0, hd, :, V_HEAD:] = jnp.ones((tm, V_HEAD), BF16)
    for g in range(FOURIER_GROUPS):
        lo = o3 + g * FOURIER_GROUP_DIM
        f_sc[g] = p[:, lo:lo + FOURIER_GROUP_DIM]
    for r in range(DFT_RADIX):
        for g in range(FOURIER_GROUPS):
            lo = g * FOURIER_GROUP_DIM
            xr_ref[r, :, lo:lo + FOURIER_GROUP_DIM] = f_sc[g, pl.ds(r, tm // DFT_RADIX, stride=DFT_RADIX), :].astype(BF16)


def _proj(x, g, w1, qn, wuq, kvn, wukv, rc, rs, tm):
    B, S, D = x.shape
    H = MLA_HEADS
    nw1 = w1.shape[1]
    grid = (B, S // tm)
    R = DFT_RADIX
    return pl.pallas_call(
        _proj_kernel,
        grid=grid,
        in_specs=[
            pl.BlockSpec((1, tm, D), lambda b, i: (b, i, 0)),
            _const_spec((1, D)),
            _const_spec((D, nw1)),
            _const_spec((1, Q_LORA)),
            _const_spec((Q_LORA, H * QK_PAD)),
            _const_spec((1, KV_LORA)),
            _const_spec((KV_LORA, H * (QK_NOPE + V_HEAD))),
            pl.BlockSpec((tm, LANES), lambda b, i: (i, 0)),
            pl.BlockSpec((tm, LANES), lambda b, i: (i, 0)),
        ],
        out_specs=[
            pl.BlockSpec((1, H, tm, QK_PAD), lambda b, i: (b, 0, i, 0)),
            pl.BlockSpec((1, H, tm, QK_PAD), lambda b, i: (b, 0, i, 0)),
            pl.BlockSpec((1, H, tm, 2 * V_HEAD), lambda b, i: (b, 0, i, 0)),
            pl.BlockSpec((R, tm // R, D_FOURIER), lambda b, i: (0, i, b)),
        ],
        out_shape=[
            jax.ShapeDtypeStruct((B, H, S, QK_PAD), BF16),
            jax.ShapeDtypeStruct((B, H, S, QK_PAD), BF16),
            jax.ShapeDtypeStruct((B, H, S, 2 * V_HEAD), BF16),
            jax.ShapeDtypeStruct((R, S // R, B * D_FOURIER), BF16),
        ],
        scratch_shapes=[pltpu.VMEM((FOURIER_GROUPS, tm, FOURIER_GROUP_DIM), F32)],
        compiler_params=_params("parallel", "parallel"),
        name="proj",
    )(x, g, w1, qn, wuq, kvn, wukv, rc, rs)


def _attn_kernel(q_ref, k_ref, v_ref, o_ref, s_sc, p_sc, a_sc, m_sc, acc_sc, *, tk, rb):
    S = k_ref.shape[2]
    tq = q_ref.shape[2]
    n = S // tk
    nc = tk // LANES
    q = q_ref[0, 0]

    def scores(j):
        s_sc[j % 2] = _dot_nt(q, k_ref[0, 0, j * tk:(j + 1) * tk, :])

    def softmax(j):
        slot = j % 2
        for r in range(tq // rb):
            rows = slice(r * rb, (r + 1) * rb)
            cols = [s_sc[slot, rows, c * LANES:(c + 1) * LANES] for c in range(nc)]
            mx = cols[0]
            for c in range(1, nc):
                mx = jnp.maximum(mx, cols[c])
            m_cur = jnp.broadcast_to(jnp.max(mx, axis=-1, keepdims=True), (rb, LANES))
            if j == 0:
                m_new = m_cur
            else:
                m_prev = m_sc[rows, :]
                m_new = jnp.maximum(m_prev, m_cur)
                a_sc[slot, rows, :] = jnp.exp2(m_prev - m_new)
            m_sc[rows, :] = m_new
            for c in range(nc):
                p_sc[slot, rows, c * LANES:(c + 1) * LANES] = jnp.exp2((cols[c] - m_new).astype(BF16))

    def values(j):
        slot = j % 2
        pv = _dot(p_sc[slot], v_ref[0, 0, j * tk:(j + 1) * tk, :])
        if j == 0:
            acc_sc[...] = pv
        else:
            alpha = a_sc[slot]
            acc_sc[...] = jnp.concatenate([alpha, alpha], axis=-1) * acc_sc[...] + pv

    scores(0)
    for j in range(n):
        if j + 1 < n:
            scores(j + 1)
        if j >= 1:
            values(j - 1)
        softmax(j)
    values(n - 1)
    o_ref[0] = (acc_sc[:, :V_HEAD] / acc_sc[:, V_HEAD:]).astype(BF16)


def _attention(q, k, v, tq, tk):
    B, H, S, _ = q.shape
    return pl.pallas_call(
        functools.partial(_attn_kernel, tk=tk, rb=min(32, tq)),
        grid=(B, H, S // tq),
        in_specs=[
            pl.BlockSpec((1, 1, tq, QK_PAD), lambda b, h, i: (b, h, i, 0)),
            pl.BlockSpec((1, 1, S, QK_PAD), lambda b, h, i: (b, h, 0, 0)),
            pl.BlockSpec((1, 1, S, 2 * V_HEAD), lambda b, h, i: (b, h, 0, 0)),
        ],
        out_specs=pl.BlockSpec((1, tq, V_HEAD), lambda b, h, i: (b, i, h)),
        out_shape=jax.ShapeDtypeStruct((B, S, H * V_HEAD), BF16),
        scratch_shapes=[
            pltpu.VMEM((2, tq, tk), F32),
            pltpu.VMEM((2, tq, tk), BF16),
            pltpu.VMEM((2, tq, LANES), F32),
            pltpu.VMEM((tq, LANES), F32),
            pltpu.VMEM((tq, 2 * V_HEAD), F32),
        ],
        compiler_params=_params("parallel", "parallel", "arbitrary"),
        name="attn",
    )(q, k, v)


def _seqdft_kernel(gc_ref, gs_ref, xr_ref, cd_ref, y_ref):
    dr, di = [], []
    for r in range(DFT_RADIX):
        x = xr_ref[r]
        dr.append(_dot(gc_ref[r], x))
        di.append(_dot(gs_ref[r], x))
    ar, br = dr[0] + dr[2], dr[0] - dr[2]
    ai, bi = di[0] + di[2], di[0] - di[2]
    cr, er = dr[1] + dr[3], dr[1] - dr[3]
    ci, ei = di[1] + di[3], di[1] - di[3]
    xre = (ar + cr, br + ei, ar - cr, br - ei)
    xim = (ai + ci, bi - er, ai - ci, bi + er)
    cd = cd_ref[...]
    for k2 in range(DFT_RADIX):
        re = xre[k2].astype(BF16)
        im = xim[k2].astype(BF16)
        for g in range(FOURIER_GROUPS):
            lo = g * FOURIER_GROUP_DIM
            z = jnp.concatenate([re[:, lo:lo + FOURIER_GROUP_DIM], im[:, lo:lo + FOURIER_GROUP_DIM]], axis=-1)
            y_ref[k2, :, lo:lo + FOURIER_GROUP_DIM] = _dot(z, cd).astype(BF16)


def _seqdft(gc, gs, xr, cd, tmr):
    R, M, N = xr.shape
    tn = D_FOURIER
    return pl.pallas_call(
        _seqdft_kernel,
        grid=(N // tn, M // tmr),
        in_specs=[
            pl.BlockSpec((R, tmr, M), lambda j, i: (0, i, 0)),
            pl.BlockSpec((R, tmr, M), lambda j, i: (0, i, 0)),
            pl.BlockSpec((R, M, tn), lambda j, i: (0, 0, j)),
            _const_spec(cd.shape),
        ],
        out_specs=pl.BlockSpec((R, tmr, tn), lambda j, i: (0, i, j)),
        out_shape=jax.ShapeDtypeStruct((R, M, N), BF16),
        compiler_params=_params("parallel", "parallel"),
        name="seqdft",
    )(gc, gs, xr, cd)


def _memkv_kernel(m_ref, g_ref, w_ref, mk_ref, mv_ref):
    h = _rms(m_ref[0], g_ref[...]).astype(BF16)
    kv = _dot(h, w_ref[...])
    mk_ref[0] = kv[:, :D_MEM].astype(BF16)
    mv_ref[0] = kv[:, D_MEM:].astype(BF16)


def _memkv(mem, g, w):
    B, M, D = mem.shape
    return pl.pallas_call(
        _memkv_kernel,
        grid=(B,),
        in_specs=[
            pl.BlockSpec((1, M, D), lambda b: (b, 0, 0)),
            _const_spec((1, D)),
            _const_spec((D, 2 * D_MEM)),
        ],
        out_specs=[
            pl.BlockSpec((1, M, D_MEM), lambda b: (b, 0, 0)),
            pl.BlockSpec((1, M, D_MEM), lambda b: (b, 0, 0)),
        ],
        out_shape=[
            jax.ShapeDtypeStruct((B, M, D_MEM), BF16),
            jax.ShapeDtypeStruct((B, M, D_MEM), BF16),
        ],
        compiler_params=_params("parallel"),
        name="memkv",
    )(mem, g, w)


def _merge_kernel(x_ref, g_ref, wg_ref, wqm_ref, att_ref, y_ref, mk_ref, mv_ref,
                  womla_ref, wf_ref, womem_ref, wout_ref, o_ref):
    x = x_ref[0]
    D = x.shape[-1]
    h = _rms(x, g_ref[...]).astype(BF16)
    qm = _dot(h, wqm_ref[...]).astype(BF16)
    mk = mk_ref[0]
    mv = mv_ref[0]
    mscale = MEM_HEAD_DIM ** -0.5
    ctx = []
    for hd in range(MEM_HEADS):
        lo = hd * MEM_HEAD_DIM
        s = _dot_nt(qm[:, lo:lo + MEM_HEAD_DIM], mk[:, lo:lo + MEM_HEAD_DIM]) * mscale
        s = s - jnp.max(s, axis=-1, keepdims=True)
        e = jnp.exp(s)
        pm = (e / jnp.sum(e, axis=-1, keepdims=True)).astype(BF16)
        ctx.append(_dot(pm, mv[:, lo:lo + MEM_HEAD_DIM]).astype(BF16))
    c = _dot(jnp.concatenate(ctx, axis=-1), womem_ref[...])
    a = _dot(att_ref[0], womla_ref[...])
    f = _dot(y_ref[...], wf_ref[...])
    merged = jax.nn.sigmoid(_dot(h, wg_ref[:, :D])) * a
    merged += jax.nn.sigmoid(_dot(h, wg_ref[:, D:2 * D])) * f
    merged += jax.nn.sigmoid(_dot(h, wg_ref[:, 2 * D:])) * c
    o_ref[0] = x + _dot(merged.astype(BF16), wout_ref[...])


def _merge(x, g, wg, wqm, att, y, mk, mv, womla, wf, womem, wout, tm):
    B, S, D = x.shape
    M = mk.shape[1]
    return pl.pallas_call(
        _merge_kernel,
        grid=(B, S // tm),
        in_specs=[
            pl.BlockSpec((1, tm, D), lambda b, i: (b, i, 0)),
            _const_spec((1, D)),
            _const_spec(wg.shape),
            _const_spec(wqm.shape),
            pl.BlockSpec((1, tm, MLA_HEADS * V_HEAD), lambda b, i: (b, i, 0)),
            pl.BlockSpec((tm, D_FOURIER), lambda b, i: (i, b)),
            pl.BlockSpec((1, M, D_MEM), lambda b, i: (b, 0, 0)),
            pl.BlockSpec((1, M, D_MEM), lambda b, i: (b, 0, 0)),
            _const_spec(womla.shape),
            _const_spec(wf.shape),
            _const_spec(womem.shape),
            _const_spec(wout.shape),
        ],
        out_specs=pl.BlockSpec((1, tm, D), lambda b, i: (b, i, 0)),
        out_shape=jax.ShapeDtypeStruct((B, S, D), F32),
        compiler_params=_params("parallel", "parallel"),
        name="merge",
    )(x, g, wg, wqm, att, y, mk, mv, womla, wf, womem, wout)


def _ffn_kernel(x_ref, g_ref, wgu_ref, wd_ref, fg_ref, o_ref, *, n_chunks, final_norm):
    x = x_ref[...]
    d_ff = wd_ref.shape[0]
    cw = d_ff // n_chunks
    h = _rms(x, g_ref[...]).astype(BF16)
    acc = x
    for c in range(n_chunks):
        gate = _dot(h, wgu_ref[:, c * cw:(c + 1) * cw])
        up = _dot(h, wgu_ref[:, d_ff + c * cw:d_ff + (c + 1) * cw])
        act = (gate * jax.nn.sigmoid(gate) * up).astype(BF16)
        acc = acc + _dot(act, wd_ref[c * cw:(c + 1) * cw, :])
    if final_norm:
        acc = _rms(acc, fg_ref[...])
    o_ref[...] = acc


def _ffn(x, g, wgu, wd, fg, tm, final_norm):
    T, D = x.shape
    d_ff = wd.shape[0]
    n_chunks = 2 if (d_ff // 2) % LANES == 0 else 1
    return pl.pallas_call(
        functools.partial(_ffn_kernel, n_chunks=n_chunks, final_norm=final_norm),
        grid=(T // tm,),
        in_specs=[
            pl.BlockSpec((tm, D), lambda i: (i, 0)),
            _const_spec((1, D)),
            _const_spec(wgu.shape),
            _const_spec(wd.shape),
            _const_spec((1, D)),
        ],
        out_specs=pl.BlockSpec((tm, D), lambda i: (i, 0)),
        out_shape=jax.ShapeDtypeStruct((T, D), F32),
        compiler_params=_params("parallel"),
        name="ffn_final" if final_norm else "ffn",
    )(x, g, wgu, wd, fg)


def _rope_tables(S):
    inv_freq = ROPE_THETA ** (-(jnp.arange(0, QK_ROPE, 2, dtype=F32) / QK_ROPE))
    ang = jnp.arange(S, dtype=F32)[:, None] * inv_freq[None, :]
    c, s, z = jnp.cos(ang), jnp.sin(ang), jnp.zeros_like(ang)
    return jnp.concatenate([c, z, c, z], axis=-1), jnp.concatenate([-s, z, s, z], axis=-1)


def _channel_dft_table():
    n = FOURIER_GROUP_DIM
    idx = jnp.arange(n, dtype=jnp.int32)
    ang = ((idx[:, None] * idx[None, :]) % n).astype(F32) * (2.0 * np.pi / n)
    return (jnp.concatenate([jnp.cos(ang), jnp.sin(ang)], axis=0) * n ** -0.5).astype(BF16)


def _seq_dft_tables(S):
    R = DFT_RADIX
    M = S // R
    n = R * jnp.arange(M, dtype=jnp.int32)[None, None, :] + jnp.arange(R, dtype=jnp.int32)[:, None, None]

    def exact(k):
        ang = ((k[None, :, None] * n) % S).astype(F32) * (2.0 * np.pi / S)
        return jnp.cos(ang), jnp.sin(ang)

    hc, hs = exact(SUBLANES * jnp.arange(M // SUBLANES, dtype=jnp.int32))
    lc, ls = exact(jnp.arange(SUBLANES, dtype=jnp.int32))
    norm = S ** -0.5
    lc, ls = lc * norm, ls * norm
    c = hc[:, :, None, :] * lc[:, None, :, :] - hs[:, :, None, :] * ls[:, None, :, :]
    s = hs[:, :, None, :] * lc[:, None, :, :] + hc[:, :, None, :] * ls[:, None, :, :]
    return c.reshape(R, M, M).astype(BF16), (-s).reshape(R, M, M).astype(BF16)


def _pad_rope_cols(w):
    half = QK_ROPE // 2
    z = jnp.zeros(w.shape[:-1] + (half,), w.dtype)
    return jnp.concatenate([w[..., :half], z, w[..., half:], z], axis=-1)


def _layer_weights(l, attn_norm, w_in, q_norm, w_uq, kv_norm, w_ukv, w_o_mla, w_fourier, mem_norm,
                   w_mem_kv, w_o_mem, w_out, ffn_norm, w_gate_up, w_down):
    o1 = Q_LORA
    o2 = o1 + KV_LORA
    o3 = o2 + QK_ROPE
    o4 = o3 + D_FOURIER
    o5 = o4 + D_MEM
    wi = w_in[l]
    w1 = jnp.concatenate([wi[:, :o2], _pad_rope_cols(wi[:, o2:o3]), wi[:, o3:o4]], axis=-1).astype(BF16)
    uq = w_uq[l].reshape(Q_LORA, MLA_HEADS, QK_NOPE + QK_ROPE)
    uq = jnp.concatenate([uq[..., :QK_NOPE], _pad_rope_cols(uq[..., QK_NOPE:])], axis=-1)
    uq = uq.reshape(Q_LORA, MLA_HEADS * QK_PAD).astype(BF16)
    return dict(
        attn_norm=attn_norm[l][None, :], w1=w1, q_norm=q_norm[l][None, :], wuq=uq,
        kv_norm=kv_norm[l][None, :], wukv=w_ukv[l].astype(BF16),
        wqm=wi[:, o4:o5].astype(BF16), wg=wi[:, o5:].astype(BF16),
        womla=w_o_mla[l].astype(BF16), wf=w_fourier[l].astype(BF16),
        mem_norm=mem_norm[l][None, :], wmemkv=w_mem_kv[l].astype(BF16), womem=w_o_mem[l].astype(BF16),
        wout=w_out[l].astype(BF16), ffn_norm=ffn_norm[l][None, :],
        wgu=w_gate_up[l].astype(BF16), wd=w_down[l].astype(BF16),
    )


def _tiles(S):
    return dict(tm=min(512, S), tq=min(1024, S), tk=min(1024, S))


def _group_layer(x, mem, w, tabs, final_gain, final_norm):
    B, S, D = x.shape
    t = _tiles(S)
    q, k, v, xr = _proj(x, w["attn_norm"], w["w1"], w["q_norm"], w["wuq"], w["kv_norm"], w["wukv"],
                        tabs["rc"], tabs["rs"], t["tm"])
    att = _attention(q, k, v, t["tq"], t["tk"])
    y = _seqdft(tabs["gc"], tabs["gs"], xr, tabs["cdft"], min(256, S // DFT_RADIX))
    y = y.reshape(S, B * D_FOURIER)
    mk, mv = _memkv(mem, w["mem_norm"], w["wmemkv"])
    x1 = _merge(x, w["attn_norm"], w["wg"], w["wqm"], att, y, mk, mv,
                w["womla"], w["wf"], w["womem"], w["wout"], t["tm"])
    x2 = _ffn(x1.reshape(B * S, D), w["ffn_norm"], w["wgu"], w["wd"], final_gain, t["tm"], final_norm)
    return x2.reshape(B, S, D)


def _group_tables(S):
    rc, rs = _rope_tables(S)
    gc, gs = _seq_dft_tables(S)
    return dict(rc=rc, rs=rs, cdft=_channel_dft_table(), gc=gc, gs=gs)


def kernel(x_prompt, x_sample, mem_prompt, mem_sample, attn_norm, w_in, q_norm, w_uq, kv_norm, w_ukv,
           w_o_mla, w_fourier, mem_norm, w_mem_kv, w_o_mem, w_out, ffn_norm, w_gate_up, w_down, final_norm):
    depth = w_in.shape[0]
    tabs_p = _group_tables(x_prompt.shape[1])
    tabs_s = _group_tables(x_sample.shape[1])
    fg = final_norm[None, :]
    yp, ys = x_prompt, x_sample
    for l in range(depth):
        w = _layer_weights(l, attn_norm, w_in, q_norm, w_uq, kv_norm, w_ukv, w_o_mla, w_fourier, mem_norm,
                           w_mem_kv, w_o_mem, w_out, ffn_norm, w_gate_up, w_down)
        last = l == depth - 1
        yp = _group_layer(yp, mem_prompt, w, tabs_p, fg, last)
        ys = _group_layer(ys, mem_sample, w, tabs_s, fg, last)
    return (yp, ys)
```

```python
import functools

import jax
import jax.numpy as jnp
import numpy as np
from jax import lax
from jax.experimental import pallas as pl
from jax.experimental.pallas import tpu as pltpu

MLA_HEADS = 8
QK_NOPE = 128
QK_ROPE = 64
V_HEAD = 128
Q_LORA = 384
KV_LORA = 256
ROPE_THETA = 10000.0
FOURIER_GROUPS = 4
FOURIER_GROUP_DIM = 128
D_FOURIER = FOURIER_GROUPS * FOURIER_GROUP_DIM
MEM_HEADS = 4
MEM_HEAD_DIM = 128
D_MEM = MEM_HEADS * MEM_HEAD_DIM
EPS = 1e-6

LANES = 128
QK_PAD = QK_NOPE + LANES
VMEM_LIMIT = 56 * 1024 * 1024
DFT_RADIX = 4
SUBLANES = 8
Q_PRESCALE = float((QK_NOPE + QK_ROPE) ** -0.5 * np.log2(np.e))

F32 = jnp.float32
BF16 = jnp.bfloat16


def _rms(x, g):
    return x * lax.rsqrt(jnp.mean(x * x, axis=-1, keepdims=True) + EPS) * g


def _dot(a, b):
    return jnp.dot(a, b, preferred_element_type=F32)


def _dot_nt(a, b):
    return lax.dot_general(a, b, (((1,), (1,)), ((), ())), preferred_element_type=F32)


def _rope(x, c, s):
    return x * c + pltpu.roll(x, LANES // 2, axis=1) * s


def _const_spec(shape):
    nd = len(shape)
    return pl.BlockSpec(shape, lambda *_: (0,) * nd, pipeline_mode=pl.Buffered(1))


def _params(*sem, flags=None):
    return pltpu.CompilerParams(dimension_semantics=sem, vmem_limit_bytes=VMEM_LIMIT, flags=flags)


def _proj_kernel(x_ref, g_ref, w1_ref, qn_ref, wuq_ref, kvn_ref, wukv_ref, rc_ref, rs_ref,
                 q_ref, k_ref, v_ref, xr_ref, f_sc):
    x = x_ref[0]
    tm = x.shape[0]
    h = _rms(x, g_ref[...]).astype(BF16)
    p = _dot(h, w1_ref[...])
    o1 = Q_LORA
    o2 = o1 + KV_LORA
    o3 = o2 + LANES
    rc = rc_ref[...]
    rs = rs_ref[...]
    cq = _rms(p[:, :o1], qn_ref[...]).astype(BF16)
    q = _dot(cq, wuq_ref[...]) * Q_PRESCALE
    ckv = _rms(p[:, o1:o2], kvn_ref[...]).astype(BF16)
    kv = _dot(ckv, wukv_ref[...])
    kpe_t = _rope(p[:, o2:o3], rc, rs).T.astype(BF16)
    for hd in range(MLA_HEADS):
        b0 = hd * QK_PAD
        q_ref[0, hd, :, :QK_NOPE] = q[:, b0:b0 + QK_NOPE].astype(BF16)
        q_ref[0, hd, :, QK_NOPE:] = _rope(q[:, b0 + QK_NOPE:b0 + QK_PAD], rc, rs).astype(BF16)
        c0 = hd * (QK_NOPE + V_HEAD)
        k_ref[0, hd, :QK_NOPE, :] = kv[:, c0:c0 + QK_NOPE].T.astype(BF16)
        k_ref[0, hd, QK_NOPE:, :] = kpe_t
        v_ref[0, hd, :, :V_HEAD] = kv[:, c0 + QK_NOPE:c0 + QK_NOPE + V_HEAD].astype(BF16)
        v_ref[0, hd, :, V_HEAD:] = jnp.ones((tm, V_HEAD), BF16)
    for g in range(FOURIER_GROUPS):
        lo = o3 + g * FOURIER_GROUP_DIM
        f_sc[g] = p[:, lo:lo + FOURIER_GROUP_DIM]
    for r in range(DFT_RADIX):
        for g in range(FOURIER_GROUPS):
            lo = g * FOURIER_GROUP_DIM
            xr_ref[r, :, lo:lo + FOURIER_GROUP_DIM] = f_sc[g, pl.ds(r, tm // DFT_RADIX, stride=DFT_RADIX), :].astype(BF16)


def _proj(x, g, w1, qn, wuq, kvn, wukv, rc, rs, tm):
    B, S, D = x.shape
    H = MLA_HEADS
    nw1 = w1.shape[1]
    grid = (B, S // tm)
    R = DFT_RADIX
    return pl.pallas_call(
        _proj_kernel,
        grid=grid,
        in_specs=[
            pl.BlockSpec((1, tm, D), lambda b, i: (b, i, 0)),
            _const_spec((1, D)),
            _const_spec((D, nw1)),
            _const_spec((1, Q_LORA)),
            _const_spec((Q_LORA, H * QK_PAD)),
            _const_spec((1, KV_LORA)),
            _const_spec((KV_LORA, H * (QK_NOPE + V_HEAD))),
            pl.BlockSpec((tm, LANES), lambda b, i: (i, 0)),
            pl.BlockSpec((tm, LANES), lambda b, i: (i, 0)),
        ],
        out_specs=[
            pl.BlockSpec((1, H, tm, QK_PAD), lambda b, i: (b, 0, i, 0)),
            pl.BlockSpec((1, H, QK_PAD, tm), lambda b, i: (b, 0, 0, i)),
            pl.BlockSpec((1, H, tm, 2 * V_HEAD), lambda b, i: (b, 0, i, 0)),
            pl.BlockSpec((R, tm // R, D_FOURIER), lambda b, i: (0, i, b)),
        ],
        out_shape=[
            jax.ShapeDtypeStruct((B, H, S, QK_PAD), BF16),
            jax.ShapeDtypeStruct((B, H, QK_PAD, S), BF16),
            jax.ShapeDtypeStruct((B, H, S, 2 * V_HEAD), BF16),
            jax.ShapeDtypeStruct((R, S // R, B * D_FOURIER), BF16),
        ],
        scratch_shapes=[pltpu.VMEM((FOURIER_GROUPS, tm, FOURIER_GROUP_DIM), F32)],
        compiler_params=_params("parallel", "parallel"),
        name="proj",
    )(x, g, w1, qn, wuq, kvn, wukv, rc, rs)


def _attn_kernel(q_ref, k_ref, v_ref, o_ref, s_sc, p_sc, a_sc, m_sc, acc_sc, *, tk, rb):
    S = k_ref.shape[3]
    tq = q_ref.shape[2]
    n = S // tk
    nc = tk // LANES
    q = q_ref[0, 0]

    def scores(j):
        s_sc[j % 2] = _dot(q, k_ref[0, 0, :, j * tk:(j + 1) * tk])

    def softmax(j):
        slot = j % 2
        for r in range(tq // rb):
            rows = slice(r * rb, (r + 1) * rb)
            cols = [s_sc[slot, rows, c * LANES:(c + 1) * LANES] for c in range(nc)]
            mx = cols[0]
            for c in range(1, nc):
                mx = jnp.maximum(mx, cols[c])
            m_cur = jnp.broadcast_to(jnp.max(mx, axis=-1, keepdims=True), (rb, LANES))
            if j == 0:
                m_new = m_cur
            else:
                m_prev = m_sc[rows, :]
                m_new = jnp.maximum(m_prev, m_cur)
                a_sc[slot, rows, :] = jnp.exp2(m_prev - m_new)
            m_sc[rows, :] = m_new
            for c in range(nc):
                p_sc[slot, rows, c * LANES:(c + 1) * LANES] = jnp.exp2((cols[c] - m_new).astype(BF16))

    def values(j):
        slot = j % 2
        pv = _dot(p_sc[slot], v_ref[0, 0, j * tk:(j + 1) * tk, :])
        if j == 0:
            acc_sc[...] = pv
        else:
            alpha = a_sc[slot]
            acc_sc[...] = jnp.concatenate([alpha, alpha], axis=-1) * acc_sc[...] + pv

    scores(0)
    for j in range(n):
        if j + 1 < n:
            scores(j + 1)
        if j >= 1:
            values(j - 1)
        softmax(j)
    values(n - 1)
    o_ref[0] = (acc_sc[:, :V_HEAD] / acc_sc[:, V_HEAD:]).astype(BF16)


def _attention(q, k, v, tq, tk):
    B, H, S, _ = q.shape
    return pl.pallas_call(
        functools.partial(_attn_kernel, tk=tk, rb=min(32, tq)),
        grid=(B, H, S // tq),
        in_specs=[
            pl.BlockSpec((1, 1, tq, QK_PAD), lambda b, h, i: (b, h, i, 0)),
            pl.BlockSpec((1, 1, QK_PAD, S), lambda b, h, i: (b, h, 0, 0)),
            pl.BlockSpec((1, 1, S, 2 * V_HEAD), lambda b, h, i: (b, h, 0, 0)),
        ],
        out_specs=pl.BlockSpec((1, tq, V_HEAD), lambda b, h, i: (b, i, h)),
        out_shape=jax.ShapeDtypeStruct((B, S, H * V_HEAD), BF16),
        scratch_shapes=[
            pltpu.VMEM((2, tq, tk), F32),
            pltpu.VMEM((2, tq, tk), BF16),
            pltpu.VMEM((2, tq, LANES), F32),
            pltpu.VMEM((tq, LANES), F32),
            pltpu.VMEM((tq, 2 * V_HEAD), F32),
        ],
        compiler_params=_params("parallel", "parallel", "arbitrary"),
        name="attn",
    )(q, k, v)


def _seqdft_kernel(gc_ref, gs_ref, xr_ref, cd_ref, y_ref):
    dr, di = [], []
    for r in range(DFT_RADIX):
        x = xr_ref[r]
        dr.append(_dot(gc_ref[r], x))
        di.append(_dot(gs_ref[r], x))
    ar, br = dr[0] + dr[2], dr[0] - dr[2]
    ai, bi = di[0] + di[2], di[0] - di[2]
    cr, er = dr[1] + dr[3], dr[1] - dr[3]
    ci, ei = di[1] + di[3], di[1] - di[3]
    xre = (ar + cr, br + ei, ar - cr, br - ei)
    xim = (ai + ci, bi - er, ai - ci, bi + er)
    cd = cd_ref[...]
    for k2 in range(DFT_RADIX):
        re = xre[k2].astype(BF16)
        im = xim[k2].astype(BF16)
        for g in range(FOURIER_GROUPS):
            lo = g * FOURIER_GROUP_DIM
            z = jnp.concatenate([re[:, lo:lo + FOURIER_GROUP_DIM], im[:, lo:lo + FOURIER_GROUP_DIM]], axis=-1)
            y_ref[k2, :, lo:lo + FOURIER_GROUP_DIM] = _dot(z, cd).astype(BF16)


def _seqdft(gc, gs, xr, cd, tmr):
    R, M, N = xr.shape
    tn = D_FOURIER
    return pl.pallas_call(
        _seqdft_kernel,
        grid=(N // tn, M // tmr),
        in_specs=[
            pl.BlockSpec((R, tmr, M), lambda j, i: (0, i, 0)),
            pl.BlockSpec((R, tmr, M), lambda j, i: (0, i, 0)),
            pl.BlockSpec((R, M, tn), lambda j, i: (0, 0, j)),
            _const_spec(cd.shape),
        ],
        out_specs=pl.BlockSpec((R, tmr, tn), lambda j, i: (0, i, j)),
        out_shape=jax.ShapeDtypeStruct((R, M, N), BF16),
        compiler_params=_params("parallel", "parallel"),
        name="seqdft",
    )(gc, gs, xr, cd)


def _memkv_kernel(m_ref, g_ref, w_ref, mk_ref, mv_ref):
    h = _rms(m_ref[0], g_ref[...]).astype(BF16)
    kv = _dot(h, w_ref[...])
    mk_ref[0] = kv[:, :D_MEM].astype(BF16)
    mv_ref[0] = kv[:, D_MEM:].astype(BF16)


def _memkv(mem, g, w):
    B, M, D = mem.shape
    return pl.pallas_call(
        _memkv_kernel,
        grid=(B,),
        in_specs=[
            pl.BlockSpec((1, M, D), lambda b: (b, 0, 0)),
            _const_spec((1, D)),
            _const_spec((D, 2 * D_MEM)),
        ],
        out_specs=[
            pl.BlockSpec((1, M, D_MEM), lambda b: (b, 0, 0)),
            pl.BlockSpec((1, M, D_MEM), lambda b: (b, 0, 0)),
        ],
        out_shape=[
            jax.ShapeDtypeStruct((B, M, D_MEM), BF16),
            jax.ShapeDtypeStruct((B, M, D_MEM), BF16),
        ],
        compiler_params=_params("parallel"),
        name="memkv",
    )(mem, g, w)


def _merge_kernel(x_ref, g_ref, wg_ref, wqm_ref, att_ref, y_ref, mk_ref, mv_ref,
                  womla_ref, wf_ref, womem_ref, wout_ref, o_ref):
    x = x_ref[0]
    D = x.shape[-1]
    h = _rms(x, g_ref[...]).astype(BF16)
    qm = _dot(h, wqm_ref[...]).astype(BF16)
    mk = mk_ref[0]
    mv = mv_ref[0]
    mscale = MEM_HEAD_DIM ** -0.5
    ctx = []
    for hd in range(MEM_HEADS):
        lo = hd * MEM_HEAD_DIM
        s = _dot_nt(qm[:, lo:lo + MEM_HEAD_DIM], mk[:, lo:lo + MEM_HEAD_DIM]) * mscale
        s = s - jnp.max(s, axis=-1, keepdims=True)
        e = jnp.exp(s)
        pm = (e / jnp.sum(e, axis=-1, keepdims=True)).astype(BF16)
        ctx.append(_dot(pm, mv[:, lo:lo + MEM_HEAD_DIM]).astype(BF16))
    c = _dot(jnp.concatenate(ctx, axis=-1), womem_ref[...])
    a = _dot(att_ref[0], womla_ref[...])
    f = _dot(y_ref[...], wf_ref[...])
    merged = jax.nn.sigmoid(_dot(h, wg_ref[:, :D])) * a
    merged += jax.nn.sigmoid(_dot(h, wg_ref[:, D:2 * D])) * f
    merged += jax.nn.sigmoid(_dot(h, wg_ref[:, 2 * D:])) * c
    o_ref[0] = x + _dot(merged.astype(BF16), wout_ref[...])


def _merge(x, g, wg, wqm, att, y, mk, mv, womla, wf, womem, wout, tm):
    B, S, D = x.shape
    M = mk.shape[1]
    return pl.pallas_call(
        _merge_kernel,
        grid=(B, S // tm),
        in_specs=[
            pl.BlockSpec((1, tm, D), lambda b, i: (b, i, 0)),
            _const_spec((1, D)),
            _const_spec(wg.shape),
            _const_spec(wqm.shape),
            pl.BlockSpec((1, tm, MLA_HEADS * V_HEAD), lambda b, i: (b, i, 0)),
            pl.BlockSpec((tm, D_FOURIER), lambda b, i: (i, b)),
            pl.BlockSpec((1, M, D_MEM), lambda b, i: (b, 0, 0)),
            pl.BlockSpec((1, M, D_MEM), lambda b, i: (b, 0, 0)),
            _const_spec(womla.shape),
            _const_spec(wf.shape),
            _const_spec(womem.shape),
            _const_spec(wout.shape),
        ],
        out_specs=pl.BlockSpec((1, tm, D), lambda b, i: (b, i, 0)),
        out_shape=jax.ShapeDtypeStruct((B, S, D), F32),
        compiler_params=_params("parallel", "parallel"),
        name="merge",
    )(x, g, wg, wqm, att, y, mk, mv, womla, wf, womem, wout)


def _ffn_kernel(x_ref, g_ref, wgu_ref, wd_ref, fg_ref, o_ref, *, n_chunks, final_norm):
    x = x_ref[...]
    d_ff = wd_ref.shape[0]
    cw = d_ff // n_chunks
    h = _rms(x, g_ref[...]).astype(BF16)
    acc = x
    for c in range(n_chunks):
        gate = _dot(h, wgu_ref[:, c * cw:(c + 1) * cw])
        up = _dot(h, wgu_ref[:, d_ff + c * cw:d_ff + (c + 1) * cw])
        act = (gate * jax.nn.sigmoid(gate) * up).astype(BF16)
        acc = acc + _dot(act, wd_ref[c * cw:(c + 1) * cw, :])
    if final_norm:
        acc = _rms(acc, fg_ref[...])
    o_ref[...] = acc


def _ffn(x, g, wgu, wd, fg, tm, final_norm):
    T, D = x.shape
    d_ff = wd.shape[0]
    n_chunks = 2 if (d_ff // 2) % LANES == 0 else 1
    return pl.pallas_call(
        functools.partial(_ffn_kernel, n_chunks=n_chunks, final_norm=final_norm),
        grid=(T // tm,),
        in_specs=[
            pl.BlockSpec((tm, D), lambda i: (i, 0)),
            _const_spec((1, D)),
            _const_spec(wgu.shape),
            _const_spec(wd.shape),
            _const_spec((1, D)),
        ],
        out_specs=pl.BlockSpec((tm, D), lambda i: (i, 0)),
        out_shape=jax.ShapeDtypeStruct((T, D), F32),
        compiler_params=_params("parallel"),
        name="ffn_final" if final_norm else "ffn",
    )(x, g, wgu, wd, fg)


def _rope_tables(S):
    inv_freq = ROPE_THETA ** (-(jnp.arange(0, QK_ROPE, 2, dtype=F32) / QK_ROPE))
    ang = jnp.arange(S, dtype=F32)[:, None] * inv_freq[None, :]
    c, s, z = jnp.cos(ang), jnp.sin(ang), jnp.zeros_like(ang)
    return jnp.concatenate([c, z, c, z], axis=-1), jnp.concatenate([-s, z, s, z], axis=-1)


def _channel_dft_table():
    n = FOURIER_GROUP_DIM
    idx = jnp.arange(n, dtype=jnp.int32)
    ang = ((idx[:, None] * idx[None, :]) % n).astype(F32) * (2.0 * np.pi / n)
    return (jnp.concatenate([jnp.cos(ang), jnp.sin(ang)], axis=0) * n ** -0.5).astype(BF16)


def _seq_dft_tables(S):
    R = DFT_RADIX
    M = S // R
    n = R * jnp.arange(M, dtype=jnp.int32)[None, None, :] + jnp.arange(R, dtype=jnp.int32)[:, None, None]

    def exact(k):
        ang = ((k[None, :, None] * n) % S).astype(F32) * (2.0 * np.pi / S)
        return jnp.cos(ang), jnp.sin(ang)

    hc, hs = exact(SUBLANES * jnp.arange(M // SUBLANES, dtype=jnp.int32))
    lc, ls = exact(jnp.arange(SUBLANES, dtype=jnp.int32))
    norm = S ** -0.5
    lc, ls = lc * norm, ls * norm
    c = hc[:, :, None, :] * lc[:, None, :, :] - hs[:, :, None, :] * ls[:, None, :, :]
    s = hs[:, :, None, :] * lc[:, None, :, :] + hc[:, :, None, :] * ls[:, None, :, :]
    return c.reshape(R, M, M).astype(BF16), (-s).reshape(R, M, M).astype(BF16)


def _pad_rope_cols(w):
    half = QK_ROPE // 2
    z = jnp.zeros(w.shape[:-1] + (half,), w.dtype)
    return jnp.concatenate([w[..., :half], z, w[..., half:], z], axis=-1)


def _layer_weights(l, attn_norm, w_in, q_norm, w_uq, kv_norm, w_ukv, w_o_mla, w_fourier, mem_norm,
                   w_mem_kv, w_o_mem, w_out, ffn_norm, w_gate_up, w_down):
    o1 = Q_LORA
    o2 = o1 + KV_LORA
    o3 = o2 + QK_ROPE
    o4 = o3 + D_FOURIER
    o5 = o4 + D_MEM
    wi = w_in[l]
    w1 = jnp.concatenate([wi[:, :o2], _pad_rope_cols(wi[:, o2:o3]), wi[:, o3:o4]], axis=-1).astype(BF16)
    uq = w_uq[l].reshape(Q_LORA, MLA_HEADS, QK_NOPE + QK_ROPE)
    uq = jnp.concatenate([uq[..., :QK_NOPE], _pad_rope_cols(uq[..., QK_NOPE:])], axis=-1)
    uq = uq.reshape(Q_LORA, MLA_HEADS * QK_PAD).astype(BF16)
    return dict(
        attn_norm=attn_norm[l][None, :], w1=w1, q_norm=q_norm[l][None, :], wuq=uq,
        kv_norm=kv_norm[l][None, :], wukv=w_ukv[l].astype(BF16),
        wqm=wi[:, o4:o5].astype(BF16), wg=wi[:, o5:].astype(BF16),
        womla=w_o_mla[l].astype(BF16), wf=w_fourier[l].astype(BF16),
        mem_norm=mem_norm[l][None, :], wmemkv=w_mem_kv[l].astype(BF16), womem=w_o_mem[l].astype(BF16),
        wout=w_out[l].astype(BF16), ffn_norm=ffn_norm[l][None, :],
        wgu=w_gate_up[l].astype(BF16), wd=w_down[l].astype(BF16),
    )


def _tiles(S):
    return dict(tm=min(512, S), tq=min(1024, S), tk=min(1024, S))


def _group_layer(x, mem, w, tabs, final_gain, final_norm):
    B, S, D = x.shape
    t = _tiles(S)
    q, k, v, xr = _proj(x, w["attn_norm"], w["w1"], w["q_norm"], w["wuq"], w["kv_norm"], w["wukv"],
                        tabs["rc"], tabs["rs"], t["tm"])
    att = _attention(q, k, v, t["tq"], t["tk"])
    y = _seqdft(tabs["gc"], tabs["gs"], xr, tabs["cdft"], min(256, S // DFT_RADIX))
    y = y.reshape(S, B * D_FOURIER)
    mk, mv = _memkv(mem, w["mem_norm"], w["wmemkv"])
    x1 = _merge(x, w["attn_norm"], w["wg"], w["wqm"], att, y, mk, mv,
                w["womla"], w["wf"], w["womem"], w["wout"], t["tm"])
    x2 = _ffn(x1.reshape(B * S, D), w["ffn_norm"], w["wgu"], w["wd"], final_gain, t["tm"], final_norm)
    return x2.reshape(B, S, D)


def _group_tables(S):
    rc, rs = _rope_tables(S)
    gc, gs = _seq_dft_tables(S)
    return dict(rc=rc, rs=rs, cdft=_channel_dft_table(), gc=gc, gs=gs)


def kernel(x_prompt, x_sample, mem_prompt, mem_sample, attn_norm, w_in, q_norm, w_uq, kv_norm, w_ukv,
           w_o_mla, w_fourier, mem_norm, w_mem_kv, w_o_mem, w_out, ffn_norm, w_gate_up, w_down, final_norm):
    depth = w_in.shape[0]
    tabs_p = _group_tables(x_prompt.shape[1])
    tabs_s = _group_tables(x_sample.shape[1])
    fg = final_norm[None, :]
    yp, ys = x_prompt, x_sample
    for l in range(depth):
        w = _layer_weights(l, attn_norm, w_in, q_norm, w_uq, kv_norm, w_ukv, w_o_mla, w_fourier, mem_norm,
                           w_mem_kv, w_o_mem, w_out, ffn_norm, w_gate_up, w_down)
        last = l == depth - 1
        yp = _group_layer(yp, mem_prompt, w, tabs_p, fg, last)
        ys = _group_layer(ys, mem_sample, w, tabs_s, fg, last)
    return (yp, ys)
```

```python
import functools

import jax
import jax.numpy as jnp
import numpy as np
from jax import lax
from jax.experimental import pallas as pl
from jax.experimental.pallas import tpu as pltpu

MLA_HEADS = 8
QK_NOPE = 128
QK_ROPE = 64
V_HEAD = 128
Q_LORA = 384
KV_LORA = 256
ROPE_THETA = 10000.0
FOURIER_GROUPS = 4
FOURIER_GROUP_DIM = 128
D_FOURIER = FOURIER_GROUPS * FOURIER_GROUP_DIM
MEM_HEADS = 4
MEM_HEAD_DIM = 128
D_MEM = MEM_HEADS * MEM_HEAD_DIM
EPS = 1e-6

LANES = 128
QK_PAD = QK_NOPE + LANES
VMEM_LIMIT = 56 * 1024 * 1024
DFT_RADIX = 4
SUBLANES = 8
Q_PRESCALE = float((QK_NOPE + QK_ROPE) ** -0.5 * np.log2(np.e))

F32 = jnp.float32
BF16 = jnp.bfloat16


def _rms(x, g):
    return x * lax.rsqrt(jnp.mean(x * x, axis=-1, keepdims=True) + EPS) * g


def _dot(a, b):
    return jnp.dot(a, b, preferred_element_type=F32)


def _dot_nt(a, b):
    return lax.dot_general(a, b, (((1,), (1,)), ((), ())), preferred_element_type=F32)


def _rope(x, c, s):
    return x * c + pltpu.roll(x, LANES // 2, axis=1) * s


def _const_spec(shape):
    nd = len(shape)
    return pl.BlockSpec(shape, lambda *_: (0,) * nd, pipeline_mode=pl.Buffered(1))


def _params(*sem, flags=None):
    return pltpu.CompilerParams(dimension_semantics=sem, vmem_limit_bytes=VMEM_LIMIT, flags=flags)


def _proj_kernel(x_ref, g_ref, w1_ref, qn_ref, wuq_ref, kvn_ref, wukv_ref, rc_ref, rs_ref,
                 q_ref, k_ref, v_ref, xr_ref, f_sc):
    x = x_ref[0]
    tm = x.shape[0]
    h = _rms(x, g_ref[...]).astype(BF16)
    p = _dot(h, w1_ref[...])
    o1 = Q_LORA
    o2 = o1 + KV_LORA
    o3 = o2 + LANES
    rc = rc_ref[...]
    rs = rs_ref[...]
    cq = _rms(p[:, :o1], qn_ref[...]).astype(BF16)
    q = _dot(cq, wuq_ref[...]) * Q_PRESCALE
    ckv = _rms(p[:, o1:o2], kvn_ref[...]).astype(BF16)
    kv = _dot(ckv, wukv_ref[...])
    kpe_t = _rope(p[:, o2:o3], rc, rs).T.astype(BF16)
    for hd in range(MLA_HEADS):
        b0 = hd * QK_PAD
        q_ref[0, hd, :, :QK_NOPE] = q[:, b0:b0 + QK_NOPE].astype(BF16)
        q_ref[0, hd, :, QK_NOPE:] = _rope(q[:, b0 + QK_NOPE:b0 + QK_PAD], rc, rs).astype(BF16)
        c0 = hd * (QK_NOPE + V_HEAD)
        k_ref[0, hd, :QK_NOPE, :] = kv[:, c0:c0 + QK_NOPE].T.astype(BF16)
        k_ref[0, hd, QK_NOPE:, :] = kpe_t
        v_ref[0, hd, :, :V_HEAD] = kv[:, c0 + QK_NOPE:c0 + QK_NOPE + V_HEAD].astype(BF16)
        v_ref[0, hd, :, V_HEAD:] = jnp.ones((tm, V_HEAD), BF16)
    for g in range(FOURIER_GROUPS):
        lo = o3 + g * FOURIER_GROUP_DIM
        f_sc[g] = p[:, lo:lo + FOURIER_GROUP_DIM]
    for r in range(DFT_RADIX):
        for g in range(FOURIER_GROUPS):
            lo = g * FOURIER_GROUP_DIM
            xr_ref[r, :, lo:lo + FOURIER_GROUP_DIM] = f_sc[g, pl.ds(r, tm // DFT_RADIX, stride=DFT_RADIX), :].astype(BF16)


def _proj(x, g, w1, qn, wuq, kvn, wukv, rc, rs, tm):
    B, S, D = x.shape
    H = MLA_HEADS
    nw1 = w1.shape[1]
    grid = (B, S // tm)
    R = DFT_RADIX
    return pl.pallas_call(
        _proj_kernel,
        grid=grid,
        in_specs=[
            pl.BlockSpec((1, tm, D), lambda b, i: (b, i, 0)),
            _const_spec((1, D)),
            _const_spec((D, nw1)),
            _const_spec((1, Q_LORA)),
            _const_spec((Q_LORA, H * QK_PAD)),
            _const_spec((1, KV_LORA)),
            _const_spec((KV_LORA, H * (QK_NOPE + V_HEAD))),
            pl.BlockSpec((tm, LANES), lambda b, i: (i, 0)),
            pl.BlockSpec((tm, LANES), lambda b, i: (i, 0)),
        ],
        out_specs=[
            pl.BlockSpec((1, H, tm, QK_PAD), lambda b, i: (b, 0, i, 0)),
            pl.BlockSpec((1, H, QK_PAD, tm), lambda b, i: (b, 0, 0, i)),
            pl.BlockSpec((1, H, tm, 2 * V_HEAD), lambda b, i: (b, 0, i, 0)),
            pl.BlockSpec((R, tm // R, D_FOURIER), lambda b, i: (0, i, b)),
        ],
        out_shape=[
            jax.ShapeDtypeStruct((B, H, S, QK_PAD), BF16),
            jax.ShapeDtypeStruct((B, H, QK_PAD, S), BF16),
            jax.ShapeDtypeStruct((B, H, S, 2 * V_HEAD), BF16),
            jax.ShapeDtypeStruct((R, S // R, B * D_FOURIER), BF16),
        ],
        scratch_shapes=[pltpu.VMEM((FOURIER_GROUPS, tm, FOURIER_GROUP_DIM), F32)],
        compiler_params=_params("parallel", "parallel"),
        name="proj",
    )(x, g, w1, qn, wuq, kvn, wukv, rc, rs)


def _attn_kernel(q_ref, k_ref, v_ref, o_ref, s_sc, p_sc, a_sc, m_sc, acc_sc, *, tk, rb):
    S = k_ref.shape[3]
    tq = q_ref.shape[2]
    edge = tk // 2 if S >= 2 * tk else tk
    bounds = [0] + list(range(edge, S - edge + 1, tk)) + [S]
    n = len(bounds) - 1
    q = q_ref[0, 0]

    def scores(j):
        lo, hi = bounds[j], bounds[j + 1]
        s_sc[j % 2, :, :hi - lo] = _dot(q, k_ref[0, 0, :, lo:hi])

    def softmax(j):
        slot = j % 2
        nc = (bounds[j + 1] - bounds[j]) // LANES
        for r in range(tq // rb):
            rows = slice(r * rb, (r + 1) * rb)
            cols = [s_sc[slot, rows, c * LANES:(c + 1) * LANES] for c in range(nc)]
            mx = cols[0]
            for c in range(1, nc):
                mx = jnp.maximum(mx, cols[c])
            m_cur = jnp.broadcast_to(jnp.max(mx, axis=-1, keepdims=True), (rb, LANES))
            if j == 0:
                m_new = m_cur
            else:
                m_prev = m_sc[rows, :]
                m_new = jnp.maximum(m_prev, m_cur)
                a_sc[slot, rows, :] = jnp.exp2(m_prev - m_new)
            m_sc[rows, :] = m_new
            for c in range(nc):
                p_sc[slot, rows, c * LANES:(c + 1) * LANES] = jnp.exp2((cols[c] - m_new).astype(BF16))

    def values(j):
        slot = j % 2
        lo, hi = bounds[j], bounds[j + 1]
        pv = _dot(p_sc[slot, :, :hi - lo], v_ref[0, 0, lo:hi, :])
        if j == 0:
            acc_sc[...] = pv
        else:
            alpha = a_sc[slot]
            acc_sc[...] = jnp.concatenate([alpha, alpha], axis=-1) * acc_sc[...] + pv

    scores(0)
    for j in range(n):
        if j + 1 < n:
            scores(j + 1)
        if j >= 1:
            values(j - 1)
        softmax(j)
    values(n - 1)
    o_ref[0] = (acc_sc[:, :V_HEAD] / acc_sc[:, V_HEAD:]).astype(BF16)


def _attention(q, k, v, tq, tk):
    B, H, S, _ = q.shape
    return pl.pallas_call(
        functools.partial(_attn_kernel, tk=tk, rb=min(32, tq)),
        grid=(B, H, S // tq),
        in_specs=[
            pl.BlockSpec((1, 1, tq, QK_PAD), lambda b, h, i: (b, h, i, 0)),
            pl.BlockSpec((1, 1, QK_PAD, S), lambda b, h, i: (b, h, 0, 0)),
            pl.BlockSpec((1, 1, S, 2 * V_HEAD), lambda b, h, i: (b, h, 0, 0)),
        ],
        out_specs=pl.BlockSpec((1, tq, V_HEAD), lambda b, h, i: (b, i, h)),
        out_shape=jax.ShapeDtypeStruct((B, S, H * V_HEAD), BF16),
        scratch_shapes=[
            pltpu.VMEM((2, tq, tk), F32),
            pltpu.VMEM((2, tq, tk), BF16),
            pltpu.VMEM((2, tq, LANES), F32),
            pltpu.VMEM((tq, LANES), F32),
            pltpu.VMEM((tq, 2 * V_HEAD), F32),
        ],
        compiler_params=_params("parallel", "parallel", "arbitrary"),
        name="attn",
    )(q, k, v)


def _seqdft_kernel(gc_ref, gs_ref, xr_ref, cd_ref, y_ref):
    dr, di = [], []
    for r in range(DFT_RADIX):
        x = xr_ref[r]
        dr.append(_dot(gc_ref[r], x))
        di.append(_dot(gs_ref[r], x))
    ar, br = dr[0] + dr[2], dr[0] - dr[2]
    ai, bi = di[0] + di[2], di[0] - di[2]
    cr, er = dr[1] + dr[3], dr[1] - dr[3]
    ci, ei = di[1] + di[3], di[1] - di[3]
    xre = (ar + cr, br + ei, ar - cr, br - ei)
    xim = (ai + ci, bi - er, ai - ci, bi + er)
    cd = cd_ref[...]
    for k2 in range(DFT_RADIX):
        re = xre[k2].astype(BF16)
        im = xim[k2].astype(BF16)
        for g in range(FOURIER_GROUPS):
            lo = g * FOURIER_GROUP_DIM
            z = jnp.concatenate([re[:, lo:lo + FOURIER_GROUP_DIM], im[:, lo:lo + FOURIER_GROUP_DIM]], axis=-1)
            y_ref[k2, :, lo:lo + FOURIER_GROUP_DIM] = _dot(z, cd).astype(BF16)


def _seqdft(gc, gs, xr, cd, tmr):
    R, M, N = xr.shape
    tn = D_FOURIER
    return pl.pallas_call(
        _seqdft_kernel,
        grid=(N // tn, M // tmr),
        in_specs=[
            pl.BlockSpec((R, tmr, M), lambda j, i: (0, i, 0)),
            pl.BlockSpec((R, tmr, M), lambda j, i: (0, i, 0)),
            pl.BlockSpec((R, M, tn), lambda j, i: (0, 0, j)),
            _const_spec(cd.shape),
        ],
        out_specs=pl.BlockSpec((R, tmr, tn), lambda j, i: (0, i, j)),
        out_shape=jax.ShapeDtypeStruct((R, M, N), BF16),
        compiler_params=_params("parallel", "parallel"),
        name="seqdft",
    )(gc, gs, xr, cd)


def _memkv_kernel(m_ref, g_ref, w_ref, mk_ref, mv_ref):
    h = _rms(m_ref[0], g_ref[...]).astype(BF16)
    kv = _dot(h, w_ref[...])
    mk_ref[0] = kv[:, :D_MEM].astype(BF16)
    mv_ref[0] = kv[:, D_MEM:].astype(BF16)


def _memkv(mem, g, w):
    B, M, D = mem.shape
    return pl.pallas_call(
        _memkv_kernel,
        grid=(B,),
        in_specs=[
            pl.BlockSpec((1, M, D), lambda b: (b, 0, 0)),
            _const_spec((1, D)),
            _const_spec((D, 2 * D_MEM)),
        ],
        out_specs=[
            pl.BlockSpec((1, M, D_MEM), lambda b: (b, 0, 0)),
            pl.BlockSpec((1, M, D_MEM), lambda b: (b, 0, 0)),
        ],
        out_shape=[
            jax.ShapeDtypeStruct((B, M, D_MEM), BF16),
            jax.ShapeDtypeStruct((B, M, D_MEM), BF16),
        ],
        compiler_params=_params("parallel"),
        name="memkv",
    )(mem, g, w)


def _merge_kernel(x_ref, g_ref, wg_ref, wqm_ref, att_ref, y_ref, mk_ref, mv_ref,
                  womla_ref, wf_ref, womem_ref, wout_ref, o_ref):
    x = x_ref[0]
    D = x.shape[-1]
    h = _rms(x, g_ref[...]).astype(BF16)
    qm = _dot(h, wqm_ref[...]).astype(BF16)
    mk = mk_ref[0]
    mv = mv_ref[0]
    mscale = MEM_HEAD_DIM ** -0.5
    ctx = []
    for hd in range(MEM_HEADS):
        lo = hd * MEM_HEAD_DIM
        s = _dot_nt(qm[:, lo:lo + MEM_HEAD_DIM], mk[:, lo:lo + MEM_HEAD_DIM]) * mscale
        s = s - jnp.max(s, axis=-1, keepdims=True)
        e = jnp.exp(s)
        pm = (e / jnp.sum(e, axis=-1, keepdims=True)).astype(BF16)
        ctx.append(_dot(pm, mv[:, lo:lo + MEM_HEAD_DIM]).astype(BF16))
    c = _dot(jnp.concatenate(ctx, axis=-1), womem_ref[...])
    a = _dot(att_ref[0], womla_ref[...])
    f = _dot(y_ref[...], wf_ref[...])
    merged = jax.nn.sigmoid(_dot(h, wg_ref[:, :D])) * a
    merged += jax.nn.sigmoid(_dot(h, wg_ref[:, D:2 * D])) * f
    merged += jax.nn.sigmoid(_dot(h, wg_ref[:, 2 * D:])) * c
    o_ref[0] = x + _dot(merged.astype(BF16), wout_ref[...])


def _merge(x, g, wg, wqm, att, y, mk, mv, womla, wf, womem, wout, tm):
    B, S, D = x.shape
    M = mk.shape[1]
    return pl.pallas_call(
        _merge_kernel,
        grid=(B, S // tm),
        in_specs=[
            pl.BlockSpec((1, tm, D), lambda b, i: (b, i, 0)),
            _const_spec((1, D)),
            _const_spec(wg.shape),
            _const_spec(wqm.shape),
            pl.BlockSpec((1, tm, MLA_HEADS * V_HEAD), lambda b, i: (b, i, 0)),
            pl.BlockSpec((tm, D_FOURIER), lambda b, i: (i, b)),
            pl.BlockSpec((1, M, D_MEM), lambda b, i: (b, 0, 0)),
            pl.BlockSpec((1, M, D_MEM), lambda b, i: (b, 0, 0)),
            _const_spec(womla.shape),
            _const_spec(wf.shape),
            _const_spec(womem.shape),
            _const_spec(wout.shape),
        ],
        out_specs=pl.BlockSpec((1, tm, D), lambda b, i: (b, i, 0)),
        out_shape=jax.ShapeDtypeStruct((B, S, D), F32),
        compiler_params=_params("parallel", "parallel"),
        name="merge",
    )(x, g, wg, wqm, att, y, mk, mv, womla, wf, womem, wout)


def _ffn_kernel(x_ref, g_ref, wgu_ref, wd_ref, fg_ref, o_ref, *, final_norm):
    x = x_ref[...]
    d_ff = wd_ref.shape[0]
    h = _rms(x, g_ref[...]).astype(BF16)
    gate = _dot(h, wgu_ref[:, :d_ff])
    up = _dot(h, wgu_ref[:, d_ff:])
    act = (gate * jax.nn.sigmoid(gate) * up).astype(BF16)
    acc = x + _dot(act, wd_ref[...])
    if final_norm:
        acc = _rms(acc, fg_ref[...])
    o_ref[...] = acc


def _ffn(x, g, wgu, wd, fg, tm, final_norm):
    T, D = x.shape
    return pl.pallas_call(
        functools.partial(_ffn_kernel, final_norm=final_norm),
        grid=(T // tm,),
        in_specs=[
            pl.BlockSpec((tm, D), lambda i: (i, 0)),
            _const_spec((1, D)),
            _const_spec(wgu.shape),
            _const_spec(wd.shape),
            _const_spec((1, D)),
        ],
        out_specs=pl.BlockSpec((tm, D), lambda i: (i, 0)),
        out_shape=jax.ShapeDtypeStruct((T, D), F32),
        compiler_params=_params("parallel"),
        name="ffn_final" if final_norm else "ffn",
    )(x, g, wgu, wd, fg)


def _rope_tables(S):
    inv_freq = ROPE_THETA ** (-(jnp.arange(0, QK_ROPE, 2, dtype=F32) / QK_ROPE))
    ang = jnp.arange(S, dtype=F32)[:, None] * inv_freq[None, :]
    c, s, z = jnp.cos(ang), jnp.sin(ang), jnp.zeros_like(ang)
    return jnp.concatenate([c, z, c, z], axis=-1), jnp.concatenate([-s, z, s, z], axis=-1)


def _channel_dft_table():
    n = FOURIER_GROUP_DIM
    idx = jnp.arange(n, dtype=jnp.int32)
    ang = ((idx[:, None] * idx[None, :]) % n).astype(F32) * (2.0 * np.pi / n)
    return (jnp.concatenate([jnp.cos(ang), jnp.sin(ang)], axis=0) * n ** -0.5).astype(BF16)


def _seq_dft_tables(S):
    R = DFT_RADIX
    M = S // R
    n = R * jnp.arange(M, dtype=jnp.int32)[None, None, :] + jnp.arange(R, dtype=jnp.int32)[:, None, None]

    def exact(k):
        ang = ((k[None, :, None] * n) % S).astype(F32) * (2.0 * np.pi / S)
        return jnp.cos(ang), jnp.sin(ang)

    hc, hs = exact(SUBLANES * jnp.arange(M // SUBLANES, dtype=jnp.int32))
    lc, ls = exact(jnp.arange(SUBLANES, dtype=jnp.int32))
    norm = S ** -0.5
    lc, ls = lc * norm, ls * norm
    c = hc[:, :, None, :] * lc[:, None, :, :] - hs[:, :, None, :] * ls[:, None, :, :]
    s = hs[:, :, None, :] * lc[:, None, :, :] + hc[:, :, None, :] * ls[:, None, :, :]
    return c.reshape(R, M, M).astype(BF16), (-s).reshape(R, M, M).astype(BF16)


def _pad_rope_cols(w):
    half = QK_ROPE // 2
    z = jnp.zeros(w.shape[:-1] + (half,), w.dtype)
    return jnp.concatenate([w[..., :half], z, w[..., half:], z], axis=-1)


def _layer_weights(l, attn_norm, w_in, q_norm, w_uq, kv_norm, w_ukv, w_o_mla, w_fourier, mem_norm,
                   w_mem_kv, w_o_mem, w_out, ffn_norm, w_gate_up, w_down):
    o1 = Q_LORA
    o2 = o1 + KV_LORA
    o3 = o2 + QK_ROPE
    o4 = o3 + D_FOURIER
    o5 = o4 + D_MEM
    wi = w_in[l]
    w1 = jnp.concatenate([wi[:, :o2], _pad_rope_cols(wi[:, o2:o3]), wi[:, o3:o4]], axis=-1).astype(BF16)
    uq = w_uq[l].reshape(Q_LORA, MLA_HEADS, QK_NOPE + QK_ROPE)
    uq = jnp.concatenate([uq[..., :QK_NOPE], _pad_rope_cols(uq[..., QK_NOPE:])], axis=-1)
    uq = uq.reshape(Q_LORA, MLA_HEADS * QK_PAD).astype(BF16)
    return dict(
        attn_norm=attn_norm[l][None, :], w1=w1, q_norm=q_norm[l][None, :], wuq=uq,
        kv_norm=kv_norm[l][None, :], wukv=w_ukv[l].astype(BF16),
        wqm=wi[:, o4:o5].astype(BF16), wg=wi[:, o5:].astype(BF16),
        womla=w_o_mla[l].astype(BF16), wf=w_fourier[l].astype(BF16),
        mem_norm=mem_norm[l][None, :], wmemkv=w_mem_kv[l].astype(BF16), womem=w_o_mem[l].astype(BF16),
        wout=w_out[l].astype(BF16), ffn_norm=ffn_norm[l][None, :],
        wgu=w_gate_up[l].astype(BF16), wd=w_down[l].astype(BF16),
    )


def _tiles(S):
    return dict(tm=min(512, S), tq=min(1024, S), tk=min(1024, S))


def _group_layer(x, mem, w, tabs, final_gain, final_norm):
    B, S, D = x.shape
    t = _tiles(S)
    q, k, v, xr = _proj(x, w["attn_norm"], w["w1"], w["q_norm"], w["wuq"], w["kv_norm"], w["wukv"],
                        tabs["rc"], tabs["rs"], t["tm"])
    att = _attention(q, k, v, t["tq"], t["tk"])
    y = _seqdft(tabs["gc"], tabs["gs"], xr, tabs["cdft"], min(256, S // DFT_RADIX))
    y = y.reshape(S, B * D_FOURIER)
    mk, mv = _memkv(mem, w["mem_norm"], w["wmemkv"])
    x1 = _merge(x, w["attn_norm"], w["wg"], w["wqm"], att, y, mk, mv,
                w["womla"], w["wf"], w["womem"], w["wout"], t["tm"])
    x2 = _ffn(x1.reshape(B * S, D), w["ffn_norm"], w["wgu"], w["wd"], final_gain, t["tm"], final_norm)
    return x2.reshape(B, S, D)


def _group_tables(S):
    rc, rs = _rope_tables(S)
    gc, gs = _seq_dft_tables(S)
    return dict(rc=rc, rs=rs, cdft=_channel_dft_table(), gc=gc, gs=gs)


def kernel(x_prompt, x_sample, mem_prompt, mem_sample, attn_norm, w_in, q_norm, w_uq, kv_norm, w_ukv,
           w_o_mla, w_fourier, mem_norm, w_mem_kv, w_o_mem, w_out, ffn_norm, w_gate_up, w_down, final_norm):
    depth = w_in.shape[0]
    tabs_p = _group_tables(x_prompt.shape[1])
    tabs_s = _group_tables(x_sample.shape[1])
    fg = final_norm[None, :]
    yp, ys = x_prompt, x_sample
    for l in range(depth):
        w = _layer_weights(l, attn_norm, w_in, q_norm, w_uq, kv_norm, w_ukv, w_o_mla, w_fourier, mem_norm,
                           w_mem_kv, w_o_mem, w_out, ffn_norm, w_gate_up, w_down)
        last = l == depth - 1
        yp = _group_layer(yp, mem_prompt, w, tabs_p, fg, last)
        ys = _group_layer(ys, mem_sample, w, tabs_s, fg, last)
    return (yp, ys)
```

```python
import functools

import jax
import jax.numpy as jnp
import numpy as np
from jax import lax
from jax.experimental import pallas as pl
from jax.experimental.pallas import tpu as pltpu

MLA_HEADS = 8
QK_NOPE = 128
QK_ROPE = 64
V_HEAD = 128
Q_LORA = 384
KV_LORA = 256
ROPE_THETA = 10000.0
FOURIER_GROUPS = 4
FOURIER_GROUP_DIM = 128
D_FOURIER = FOURIER_GROUPS * FOURIER_GROUP_DIM
MEM_HEADS = 4
MEM_HEAD_DIM = 128
D_MEM = MEM_HEADS * MEM_HEAD_DIM
EPS = 1e-6

LANES = 128
QK_PAD = QK_NOPE + LANES
VMEM_LIMIT = 56 * 1024 * 1024
DFT_RADIX = 4
SUBLANES = 8
MAX_JUMP = 8.0
Q_PRESCALE = float((QK_NOPE + QK_ROPE) ** -0.5 * np.log2(np.e))

F32 = jnp.float32
BF16 = jnp.bfloat16


def _rms(x, g):
    return x * lax.rsqrt(jnp.mean(x * x, axis=-1, keepdims=True) + EPS) * g


def _dot(a, b):
    return jnp.dot(a, b, preferred_element_type=F32)


def _dot_nt(a, b):
    return lax.dot_general(a, b, (((1,), (1,)), ((), ())), preferred_element_type=F32)


def _rope(x, c, s):
    return x * c + pltpu.roll(x, LANES // 2, axis=1) * s


def _const_spec(shape):
    nd = len(shape)
    return pl.BlockSpec(shape, lambda *_: (0,) * nd, pipeline_mode=pl.Buffered(1))


def _params(*sem, flags=None):
    return pltpu.CompilerParams(dimension_semantics=sem, vmem_limit_bytes=VMEM_LIMIT, flags=flags)


def _proj_kernel(x_ref, g_ref, w1_ref, qn_ref, wuq_ref, kvn_ref, wukv_ref, rc_ref, rs_ref,
                 q_ref, k_ref, v_ref, xr_ref, f_sc):
    x = x_ref[0]
    tm = x.shape[0]
    h = _rms(x, g_ref[...]).astype(BF16)
    p = _dot(h, w1_ref[...])
    o1 = Q_LORA
    o2 = o1 + KV_LORA
    o3 = o2 + LANES
    rc = rc_ref[...]
    rs = rs_ref[...]
    cq = _rms(p[:, :o1], qn_ref[...]).astype(BF16)
    q = _dot(cq, wuq_ref[...]) * Q_PRESCALE
    ckv = _rms(p[:, o1:o2], kvn_ref[...]).astype(BF16)
    kv = _dot(ckv, wukv_ref[...])
    kpe_t = _rope(p[:, o2:o3], rc, rs).T.astype(BF16)
    for hd in range(MLA_HEADS):
        b0 = hd * QK_PAD
        q_ref[0, hd, :, :QK_NOPE] = q[:, b0:b0 + QK_NOPE].astype(BF16)
        q_ref[0, hd, :, QK_NOPE:] = _rope(q[:, b0 + QK_NOPE:b0 + QK_PAD], rc, rs).astype(BF16)
        c0 = hd * (QK_NOPE + V_HEAD)
        k_ref[0, hd, :QK_NOPE, :] = kv[:, c0:c0 + QK_NOPE].T.astype(BF16)
        k_ref[0, hd, QK_NOPE:, :] = kpe_t
        v_ref[0, hd, :, :V_HEAD] = kv[:, c0 + QK_NOPE:c0 + QK_NOPE + V_HEAD].astype(BF16)
        v_ref[0, hd, :, V_HEAD:] = jnp.ones((tm, V_HEAD), BF16)
    for g in range(FOURIER_GROUPS):
        lo = o3 + g * FOURIER_GROUP_DIM
        f_sc[g] = p[:, lo:lo + FOURIER_GROUP_DIM]
    for r in range(DFT_RADIX):
        for g in range(FOURIER_GROUPS):
            lo = g * FOURIER_GROUP_DIM
            xr_ref[r, :, lo:lo + FOURIER_GROUP_DIM] = f_sc[g, pl.ds(r, tm // DFT_RADIX, stride=DFT_RADIX), :].astype(BF16)


def _proj(x, g, w1, qn, wuq, kvn, wukv, rc, rs, tm):
    B, S, D = x.shape
    H = MLA_HEADS
    nw1 = w1.shape[1]
    grid = (B, S // tm)
    R = DFT_RADIX
    return pl.pallas_call(
        _proj_kernel,
        grid=grid,
        in_specs=[
            pl.BlockSpec((1, tm, D), lambda b, i: (b, i, 0)),
            _const_spec((1, D)),
            _const_spec((D, nw1)),
            _const_spec((1, Q_LORA)),
            _const_spec((Q_LORA, H * QK_PAD)),
            _const_spec((1, KV_LORA)),
            _const_spec((KV_LORA, H * (QK_NOPE + V_HEAD))),
            pl.BlockSpec((tm, LANES), lambda b, i: (i, 0)),
            pl.BlockSpec((tm, LANES), lambda b, i: (i, 0)),
        ],
        out_specs=[
            pl.BlockSpec((1, H, tm, QK_PAD), lambda b, i: (b, 0, i, 0)),
            pl.BlockSpec((1, H, QK_PAD, tm), lambda b, i: (b, 0, 0, i)),
            pl.BlockSpec((1, H, tm, 2 * V_HEAD), lambda b, i: (b, 0, i, 0)),
            pl.BlockSpec((R, tm // R, D_FOURIER), lambda b, i: (0, i, b)),
        ],
        out_shape=[
            jax.ShapeDtypeStruct((B, H, S, QK_PAD), BF16),
            jax.ShapeDtypeStruct((B, H, QK_PAD, S), BF16),
            jax.ShapeDtypeStruct((B, H, S, 2 * V_HEAD), BF16),
            jax.ShapeDtypeStruct((R, S // R, B * D_FOURIER), BF16),
        ],
        scratch_shapes=[pltpu.VMEM((FOURIER_GROUPS, tm, FOURIER_GROUP_DIM), F32)],
        compiler_params=_params("parallel", "parallel"),
        name="proj",
    )(x, g, w1, qn, wuq, kvn, wukv, rc, rs)


def _attn_kernel(q_ref, k_ref, v_ref, o_ref, s_sc, p_sc, a_sc, m_sc, acc_sc, *, tk, rb):
    S = k_ref.shape[3]
    tq = q_ref.shape[2]
    edge = tk // 2 if S >= 2 * tk else tk
    bounds = [0] + list(range(edge, S - edge + 1, tk)) + [S]
    n = len(bounds) - 1
    q = q_ref[0, 0]

    def scores(j):
        lo, hi = bounds[j], bounds[j + 1]
        s_sc[j % 2, :, :hi - lo] = _dot(q, k_ref[0, 0, :, lo:hi])

    def softmax(j):
        slot = j % 2
        nc = (bounds[j + 1] - bounds[j]) // LANES
        for r in range(tq // rb):
            rows = slice(r * rb, (r + 1) * rb)
            cols = [s_sc[slot, rows, c * LANES:(c + 1) * LANES] for c in range(nc)]
            mx = cols[0]
            for c in range(1, nc):
                mx = jnp.maximum(mx, cols[c])
            m_cur = jnp.broadcast_to(jnp.max(mx, axis=-1, keepdims=True), (rb, LANES))
            if j == 0:
                m_new = m_cur
            else:
                m_prev = m_sc[rows, :]
                m_new = jnp.maximum(m_prev, m_cur)
                a_sc[slot, rows, :] = jnp.exp2(m_prev - m_new)
            m_sc[rows, :] = m_new
            for c in range(nc):
                p_sc[slot, rows, c * LANES:(c + 1) * LANES] = jnp.exp2((cols[c] - m_new).astype(BF16))

    def values(j, lagged):
        slot = j % 2
        lo, hi = bounds[j], bounds[j + 1]
        pv = _dot(p_sc[slot, :, :hi - lo], v_ref[0, 0, lo:hi, :])
        if j == 0:
            acc_sc[...] = pv
            return None
        alpha = a_sc[slot]
        alpha2 = jnp.concatenate([alpha, alpha], axis=-1)
        acc_sc[...] = alpha2 * (acc_sc[...] + pv) if lagged else alpha2 * acc_sc[...] + pv
        return jnp.min(alpha, axis=0, keepdims=True)

    def fused(j):
        slot = j % 2
        lo, hi = bounds[j], bounds[j + 1]
        nc = (hi - lo) // LANES
        s = _dot(q, k_ref[0, 0, :, lo:hi])
        for r in range(tq // rb):
            rows = slice(r * rb, (r + 1) * rb)
            m_stab = m_sc[rows, :]
            cols = [s[rows, c * LANES:(c + 1) * LANES] for c in range(nc)]
            mx = cols[0]
            for c in range(1, nc):
                mx = jnp.maximum(mx, cols[c])
            m_cur = jnp.broadcast_to(jnp.max(mx, axis=-1, keepdims=True), (rb, LANES))
            m_new = jnp.maximum(m_stab, m_cur)
            a_sc[slot, rows, :] = jnp.exp2(m_stab - m_new)
            m_sc[rows, :] = m_new
            for c in range(nc):
                p_sc[slot, rows, c * LANES:(c + 1) * LANES] = jnp.exp2((cols[c] - m_stab).astype(BF16))

    def finish():
        o_ref[0] = (acc_sc[:, :V_HEAD] / acc_sc[:, V_HEAD:]).astype(BF16)

    scores(0)
    softmax(0)
    amin = None
    for j in range(1, n + 1):
        a = values(j - 1, True)
        if a is not None:
            amin = a if amin is None else jnp.minimum(amin, a)
        if j < n:
            fused(j)
    finish()
    if amin is not None:
        @pl.when(jnp.min(amin) < 2.0 ** -MAX_JUMP)
        def _():
            scores(0)
            for j in range(n):
                if j + 1 < n:
                    scores(j + 1)
                if j >= 1:
                    values(j - 1, False)
                softmax(j)
            values(n - 1, False)
            finish()


def _attention(q, k, v, tq, tk):
    B, H, S, _ = q.shape
    return pl.pallas_call(
        functools.partial(_attn_kernel, tk=tk, rb=min(32, tq)),
        grid=(B, H, S // tq),
        in_specs=[
            pl.BlockSpec((1, 1, tq, QK_PAD), lambda b, h, i: (b, h, i, 0)),
            pl.BlockSpec((1, 1, QK_PAD, S), lambda b, h, i: (b, h, 0, 0)),
            pl.BlockSpec((1, 1, S, 2 * V_HEAD), lambda b, h, i: (b, h, 0, 0)),
        ],
        out_specs=pl.BlockSpec((1, tq, V_HEAD), lambda b, h, i: (b, i, h)),
        out_shape=jax.ShapeDtypeStruct((B, S, H * V_HEAD), BF16),
        scratch_shapes=[
            pltpu.VMEM((2, tq, tk), F32),
            pltpu.VMEM((2, tq, tk), BF16),
            pltpu.VMEM((2, tq, LANES), F32),
            pltpu.VMEM((tq, LANES), F32),
            pltpu.VMEM((tq, 2 * V_HEAD), F32),
        ],
        compiler_params=_params("parallel", "parallel", "arbitrary"),
        name="attn",
    )(q, k, v)


def _seqdft_kernel(gc_ref, gs_ref, xr_ref, cd_ref, y_ref):
    dr, di = [], []
    for r in range(DFT_RADIX):
        x = xr_ref[r]
        dr.append(_dot(gc_ref[r], x))
        di.append(_dot(gs_ref[r], x))
    ar, br = dr[0] + dr[2], dr[0] - dr[2]
    ai, bi = di[0] + di[2], di[0] - di[2]
    cr, er = dr[1] + dr[3], dr[1] - dr[3]
    ci, ei = di[1] + di[3], di[1] - di[3]
    xre = (ar + cr, br + ei, ar - cr, br - ei)
    xim = (ai + ci, bi - er, ai - ci, bi + er)
    cd = cd_ref[...]
    for k2 in range(DFT_RADIX):
        re = xre[k2].astype(BF16)
        im = xim[k2].astype(BF16)
        for g in range(FOURIER_GROUPS):
            lo = g * FOURIER_GROUP_DIM
            z = jnp.concatenate([re[:, lo:lo + FOURIER_GROUP_DIM], im[:, lo:lo + FOURIER_GROUP_DIM]], axis=-1)
            y_ref[k2, :, lo:lo + FOURIER_GROUP_DIM] = _dot(z, cd).astype(BF16)


def _seqdft(gc, gs, xr, cd, tmr):
    R, M, N = xr.shape
    tn = D_FOURIER
    return pl.pallas_call(
        _seqdft_kernel,
        grid=(N // tn, M // tmr),
        in_specs=[
            pl.BlockSpec((R, tmr, M), lambda j, i: (0, i, 0)),
            pl.BlockSpec((R, tmr, M), lambda j, i: (0, i, 0)),
            pl.BlockSpec((R, M, tn), lambda j, i: (0, 0, j)),
            _const_spec(cd.shape),
        ],
        out_specs=pl.BlockSpec((R, tmr, tn), lambda j, i: (0, i, j)),
        out_shape=jax.ShapeDtypeStruct((R, M, N), BF16),
        compiler_params=_params("parallel", "parallel"),
        name="seqdft",
    )(gc, gs, xr, cd)


def _memkv_kernel(m_ref, g_ref, w_ref, mk_ref, mv_ref):
    h = _rms(m_ref[0], g_ref[...]).astype(BF16)
    kv = _dot(h, w_ref[...])
    mk_ref[0] = kv[:, :D_MEM].astype(BF16)
    mv_ref[0] = kv[:, D_MEM:].astype(BF16)


def _memkv(mem, g, w):
    B, M, D = mem.shape
    return pl.pallas_call(
        _memkv_kernel,
        grid=(B,),
        in_specs=[
            pl.BlockSpec((1, M, D), lambda b: (b, 0, 0)),
            _const_spec((1, D)),
            _const_spec((D, 2 * D_MEM)),
        ],
        out_specs=[
            pl.BlockSpec((1, M, D_MEM), lambda b: (b, 0, 0)),
            pl.BlockSpec((1, M, D_MEM), lambda b: (b, 0, 0)),
        ],
        out_shape=[
            jax.ShapeDtypeStruct((B, M, D_MEM), BF16),
            jax.ShapeDtypeStruct((B, M, D_MEM), BF16),
        ],
        compiler_params=_params("parallel"),
        name="memkv",
    )(mem, g, w)


def _merge_kernel(x_ref, g_ref, wg_ref, wqm_ref, att_ref, y_ref, mk_ref, mv_ref,
                  womla_ref, wf_ref, womem_ref, wout_ref, o_ref):
    x = x_ref[0]
    D = x.shape[-1]
    h = _rms(x, g_ref[...]).astype(BF16)
    qm = _dot(h, wqm_ref[...]).astype(BF16)
    mk = mk_ref[0]
    mv = mv_ref[0]
    mscale = MEM_HEAD_DIM ** -0.5
    ctx = []
    for hd in range(MEM_HEADS):
        lo = hd * MEM_HEAD_DIM
        s = _dot_nt(qm[:, lo:lo + MEM_HEAD_DIM], mk[:, lo:lo + MEM_HEAD_DIM]) * mscale
        s = s - jnp.max(s, axis=-1, keepdims=True)
        e = jnp.exp(s)
        pm = (e / jnp.sum(e, axis=-1, keepdims=True)).astype(BF16)
        ctx.append(_dot(pm, mv[:, lo:lo + MEM_HEAD_DIM]).astype(BF16))
    c = _dot(jnp.concatenate(ctx, axis=-1), womem_ref[...])
    a = _dot(att_ref[0], womla_ref[...])
    f = _dot(y_ref[...], wf_ref[...])
    merged = jax.nn.sigmoid(_dot(h, wg_ref[:, :D])) * a
    merged += jax.nn.sigmoid(_dot(h, wg_ref[:, D:2 * D])) * f
    merged += jax.nn.sigmoid(_dot(h, wg_ref[:, 2 * D:])) * c
    o_ref[0] = x + _dot(merged.astype(BF16), wout_ref[...])


def _merge(x, g, wg, wqm, att, y, mk, mv, womla, wf, womem, wout, tm):
    B, S, D = x.shape
    M = mk.shape[1]
    return pl.pallas_call(
        _merge_kernel,
        grid=(B, S // tm),
        in_specs=[
            pl.BlockSpec((1, tm, D), lambda b, i: (b, i, 0)),
            _const_spec((1, D)),
            _const_spec(wg.shape),
            _const_spec(wqm.shape),
            pl.BlockSpec((1, tm, MLA_HEADS * V_HEAD), lambda b, i: (b, i, 0)),
            pl.BlockSpec((tm, D_FOURIER), lambda b, i: (i, b)),
            pl.BlockSpec((1, M, D_MEM), lambda b, i: (b, 0, 0)),
            pl.BlockSpec((1, M, D_MEM), lambda b, i: (b, 0, 0)),
            _const_spec(womla.shape),
            _const_spec(wf.shape),
            _const_spec(womem.shape),
            _const_spec(wout.shape),
        ],
        out_specs=pl.BlockSpec((1, tm, D), lambda b, i: (b, i, 0)),
        out_shape=jax.ShapeDtypeStruct((B, S, D), F32),
        compiler_params=_params("parallel", "parallel"),
        name="merge",
    )(x, g, wg, wqm, att, y, mk, mv, womla, wf, womem, wout)


def _ffn_kernel(x_ref, g_ref, wgu_ref, wd_ref, fg_ref, o_ref, *, final_norm):
    x = x_ref[...]
    d_ff = wd_ref.shape[0]
    h = _rms(x, g_ref[...]).astype(BF16)
    gate = _dot(h, wgu_ref[:, :d_ff])
    up = _dot(h, wgu_ref[:, d_ff:])
    act = (gate * jax.nn.sigmoid(gate) * up).astype(BF16)
    acc = x + _dot(act, wd_ref[...])
    if final_norm:
        acc = _rms(acc, fg_ref[...])
    o_ref[...] = acc


def _ffn(x, g, wgu, wd, fg, tm, final_norm):
    T, D = x.shape
    return pl.pallas_call(
        functools.partial(_ffn_kernel, final_norm=final_norm),
        grid=(T // tm,),
        in_specs=[
            pl.BlockSpec((tm, D), lambda i: (i, 0)),
            _const_spec((1, D)),
            _const_spec(wgu.shape),
            _const_spec(wd.shape),
            _const_spec((1, D)),
        ],
        out_specs=pl.BlockSpec((tm, D), lambda i: (i, 0)),
        out_shape=jax.ShapeDtypeStruct((T, D), F32),
        compiler_params=_params("parallel"),
        name="ffn_final" if final_norm else "ffn",
    )(x, g, wgu, wd, fg)


def _rope_tables(S):
    inv_freq = ROPE_THETA ** (-(jnp.arange(0, QK_ROPE, 2, dtype=F32) / QK_ROPE))
    ang = jnp.arange(S, dtype=F32)[:, None] * inv_freq[None, :]
    c, s, z = jnp.cos(ang), jnp.sin(ang), jnp.zeros_like(ang)
    return jnp.concatenate([c, z, c, z], axis=-1), jnp.concatenate([-s, z, s, z], axis=-1)


def _channel_dft_table():
    n = FOURIER_GROUP_DIM
    idx = jnp.arange(n, dtype=jnp.int32)
    ang = ((idx[:, None] * idx[None, :]) % n).astype(F32) * (2.0 * np.pi / n)
    return (jnp.concatenate([jnp.cos(ang), jnp.sin(ang)], axis=0) * n ** -0.5).astype(BF16)


def _seq_dft_tables(S):
    R = DFT_RADIX
    M = S // R
    n = R * jnp.arange(M, dtype=jnp.int32)[None, None, :] + jnp.arange(R, dtype=jnp.int32)[:, None, None]

    def exact(k):
        ang = ((k[None, :, None] * n) % S).astype(F32) * (2.0 * np.pi / S)
        return jnp.cos(ang), jnp.sin(ang)

    hc, hs = exact(SUBLANES * jnp.arange(M // SUBLANES, dtype=jnp.int32))
    lc, ls = exact(jnp.arange(SUBLANES, dtype=jnp.int32))
    norm = S ** -0.5
    lc, ls = lc * norm, ls * norm
    c = hc[:, :, None, :] * lc[:, None, :, :] - hs[:, :, None, :] * ls[:, None, :, :]
    s = hs[:, :, None, :] * lc[:, None, :, :] + hc[:, :, None, :] * ls[:, None, :, :]
    return c.reshape(R, M, M).astype(BF16), (-s).reshape(R, M, M).astype(BF16)


def _pad_rope_cols(w):
    half = QK_ROPE // 2
    z = jnp.zeros(w.shape[:-1] + (half,), w.dtype)
    return jnp.concatenate([w[..., :half], z, w[..., half:], z], axis=-1)


def _layer_weights(l, attn_norm, w_in, q_norm, w_uq, kv_norm, w_ukv, w_o_mla, w_fourier, mem_norm,
                   w_mem_kv, w_o_mem, w_out, ffn_norm, w_gate_up, w_down):
    o1 = Q_LORA
    o2 = o1 + KV_LORA
    o3 = o2 + QK_ROPE
    o4 = o3 + D_FOURIER
    o5 = o4 + D_MEM
    wi = w_in[l]
    w1 = jnp.concatenate([wi[:, :o2], _pad_rope_cols(wi[:, o2:o3]), wi[:, o3:o4]], axis=-1).astype(BF16)
    uq = w_uq[l].reshape(Q_LORA, MLA_HEADS, QK_NOPE + QK_ROPE)
    uq = jnp.concatenate([uq[..., :QK_NOPE], _pad_rope_cols(uq[..., QK_NOPE:])], axis=-1)
    uq = uq.reshape(Q_LORA, MLA_HEADS * QK_PAD).astype(BF16)
    return dict(
        attn_norm=attn_norm[l][None, :], w1=w1, q_norm=q_norm[l][None, :], wuq=uq,
        kv_norm=kv_norm[l][None, :], wukv=w_ukv[l].astype(BF16),
        wqm=wi[:, o4:o5].astype(BF16), wg=wi[:, o5:].astype(BF16),
        womla=w_o_mla[l].astype(BF16), wf=w_fourier[l].astype(BF16),
        mem_norm=mem_norm[l][None, :], wmemkv=w_mem_kv[l].astype(BF16), womem=w_o_mem[l].astype(BF16),
        wout=w_out[l].astype(BF16), ffn_norm=ffn_norm[l][None, :],
        wgu=w_gate_up[l].astype(BF16), wd=w_down[l].astype(BF16),
    )


def _tiles(S):
    return dict(tm=min(512, S), tq=min(1024, S), tk=min(1024, S))


def _group_layer(x, mem, w, tabs, final_gain, final_norm):
    B, S, D = x.shape
    t = _tiles(S)
    q, k, v, xr = _proj(x, w["attn_norm"], w["w1"], w["q_norm"], w["wuq"], w["kv_norm"], w["wukv"],
                        tabs["rc"], tabs["rs"], t["tm"])
    att = _attention(q, k, v, t["tq"], t["tk"])
    y = _seqdft(tabs["gc"], tabs["gs"], xr, tabs["cdft"], min(256, S // DFT_RADIX))
    y = y.reshape(S, B * D_FOURIER)
    mk, mv = _memkv(mem, w["mem_norm"], w["wmemkv"])
    x1 = _merge(x, w["attn_norm"], w["wg"], w["wqm"], att, y, mk, mv,
                w["womla"], w["wf"], w["womem"], w["wout"], t["tm"])
    x2 = _ffn(x1.reshape(B * S, D), w["ffn_norm"], w["wgu"], w["wd"], final_gain, t["tm"], final_norm)
    return x2.reshape(B, S, D)


def _group_tables(S):
    rc, rs = _rope_tables(S)
    gc, gs = _seq_dft_tables(S)
    return dict(rc=rc, rs=rs, cdft=_channel_dft_table(), gc=gc, gs=gs)


def kernel(x_prompt, x_sample, mem_prompt, mem_sample, attn_norm, w_in, q_norm, w_uq, kv_norm, w_ukv,
           w_o_mla, w_fourier, mem_norm, w_mem_kv, w_o_mem, w_out, ffn_norm, w_gate_up, w_down, final_norm):
    depth = w_in.shape[0]
    tabs_p = _group_tables(x_prompt.shape[1])
    tabs_s = _group_tables(x_sample.shape[1])
    fg = final_norm[None, :]
    yp, ys = x_prompt, x_sample
    for l in range(depth):
        w = _layer_weights(l, attn_norm, w_in, q_norm, w_uq, kv_norm, w_ukv, w_o_mla, w_fourier, mem_norm,
                           w_mem_kv, w_o_mem, w_out, ffn_norm, w_gate_up, w_down)
        last = l == depth - 1
        yp = _group_layer(yp, mem_prompt, w, tabs_p, fg, last)
        ys = _group_layer(ys, mem_sample, w, tabs_s, fg, last)
    return (yp, ys)
```

```python
import functools

import jax
import jax.numpy as jnp
import numpy as np
from jax import lax
from jax.experimental import pallas as pl
from jax.experimental.pallas import tpu as pltpu

MLA_HEADS = 8
QK_NOPE = 128
QK_ROPE = 64
V_HEAD = 128
Q_LORA = 384
KV_LORA = 256
ROPE_THETA = 10000.0
FOURIER_GROUPS = 4
FOURIER_GROUP_DIM = 128
D_FOURIER = FOURIER_GROUPS * FOURIER_GROUP_DIM
MEM_HEADS = 4
MEM_HEAD_DIM = 128
D_MEM = MEM_HEADS * MEM_HEAD_DIM
EPS = 1e-6

LANES = 128
QK_PAD = QK_NOPE + LANES
VMEM_LIMIT = 56 * 1024 * 1024
DFT_RADIX = 4
SUBLANES = 8
MAX_JUMP = 8.0
Q_PRESCALE = float((QK_NOPE + QK_ROPE) ** -0.5 * np.log2(np.e))

F32 = jnp.float32
BF16 = jnp.bfloat16


def _rms(x, g):
    return x * lax.rsqrt(jnp.mean(x * x, axis=-1, keepdims=True) + EPS) * g


def _dot(a, b):
    return jnp.dot(a, b, preferred_element_type=F32)


def _dot_nt(a, b):
    return lax.dot_general(a, b, (((1,), (1,)), ((), ())), preferred_element_type=F32)


def _rope(x, c, s):
    return x * c + pltpu.roll(x, LANES // 2, axis=1) * s


def _const_spec(shape):
    nd = len(shape)
    return pl.BlockSpec(shape, lambda *_: (0,) * nd, pipeline_mode=pl.Buffered(1))


def _params(*sem, flags=None):
    return pltpu.CompilerParams(dimension_semantics=sem, vmem_limit_bytes=VMEM_LIMIT, flags=flags)


def _proj_kernel(x_ref, g_ref, w1_ref, qn_ref, wuq_ref, kvn_ref, wukv_ref, rc_ref, rs_ref,
                 q_ref, k_ref, v_ref, xr_ref, f_sc):
    x = x_ref[0]
    tm = x.shape[0]
    h = _rms(x, g_ref[...]).astype(BF16)
    p = _dot(h, w1_ref[...])
    o1 = Q_LORA
    o2 = o1 + KV_LORA
    o3 = o2 + LANES
    rc = rc_ref[...]
    rs = rs_ref[...]
    cq = _rms(p[:, :o1], qn_ref[...]).astype(BF16)
    q = _dot(cq, wuq_ref[...]) * Q_PRESCALE
    ckv = _rms(p[:, o1:o2], kvn_ref[...]).astype(BF16)
    kv = _dot(ckv, wukv_ref[...])
    kpe_t = _rope(p[:, o2:o3], rc, rs).T.astype(BF16)
    for hd in range(MLA_HEADS):
        b0 = hd * QK_PAD
        q_ref[0, hd, :, :QK_NOPE] = q[:, b0:b0 + QK_NOPE].astype(BF16)
        q_ref[0, hd, :, QK_NOPE:] = _rope(q[:, b0 + QK_NOPE:b0 + QK_PAD], rc, rs).astype(BF16)
        c0 = hd * (QK_NOPE + V_HEAD)
        k_ref[0, hd, :QK_NOPE, :] = kv[:, c0:c0 + QK_NOPE].T.astype(BF16)
        k_ref[0, hd, QK_NOPE:, :] = kpe_t
        v_ref[0, hd, :, :V_HEAD] = kv[:, c0 + QK_NOPE:c0 + QK_NOPE + V_HEAD].astype(BF16)
        v_ref[0, hd, :, V_HEAD:] = jnp.ones((tm, V_HEAD), BF16)
    for g in range(FOURIER_GROUPS):
        lo = o3 + g * FOURIER_GROUP_DIM
        f_sc[g] = p[:, lo:lo + FOURIER_GROUP_DIM]
    for r in range(DFT_RADIX):
        for g in range(FOURIER_GROUPS):
            lo = g * FOURIER_GROUP_DIM
            xr_ref[r, :, lo:lo + FOURIER_GROUP_DIM] = f_sc[g, pl.ds(r, tm // DFT_RADIX, stride=DFT_RADIX), :].astype(BF16)


def _proj(x, g, w1, qn, wuq, kvn, wukv, rc, rs, tm):
    B, S, D = x.shape
    H = MLA_HEADS
    nw1 = w1.shape[1]
    grid = (B, S // tm)
    R = DFT_RADIX
    return pl.pallas_call(
        _proj_kernel,
        grid=grid,
        in_specs=[
            pl.BlockSpec((1, tm, D), lambda b, i: (b, i, 0)),
            _const_spec((1, D)),
            _const_spec((D, nw1)),
            _const_spec((1, Q_LORA)),
            _const_spec((Q_LORA, H * QK_PAD)),
            _const_spec((1, KV_LORA)),
            _const_spec((KV_LORA, H * (QK_NOPE + V_HEAD))),
            pl.BlockSpec((tm, LANES), lambda b, i: (i, 0)),
            pl.BlockSpec((tm, LANES), lambda b, i: (i, 0)),
        ],
        out_specs=[
            pl.BlockSpec((1, H, tm, QK_PAD), lambda b, i: (b, 0, i, 0)),
            pl.BlockSpec((1, H, QK_PAD, tm), lambda b, i: (b, 0, 0, i)),
            pl.BlockSpec((1, H, tm, 2 * V_HEAD), lambda b, i: (b, 0, i, 0)),
            pl.BlockSpec((R, tm // R, D_FOURIER), lambda b, i: (0, i, b)),
        ],
        out_shape=[
            jax.ShapeDtypeStruct((B, H, S, QK_PAD), BF16),
            jax.ShapeDtypeStruct((B, H, QK_PAD, S), BF16),
            jax.ShapeDtypeStruct((B, H, S, 2 * V_HEAD), BF16),
            jax.ShapeDtypeStruct((R, S // R, B * D_FOURIER), BF16),
        ],
        scratch_shapes=[pltpu.VMEM((FOURIER_GROUPS, tm, FOURIER_GROUP_DIM), F32)],
        compiler_params=_params("parallel", "parallel"),
        name="proj",
    )(x, g, w1, qn, wuq, kvn, wukv, rc, rs)


def _attn_kernel(q_ref, k_ref, v_ref, o_ref, s_sc, p_sc, a_sc, m_sc, acc_sc, *, tk, rb):
    S = k_ref.shape[3]
    tq = q_ref.shape[2]
    edge = tk // 2 if S >= 2 * tk else tk
    bounds = [0] + list(range(edge, S - edge + 1, tk)) + [S]
    n = len(bounds) - 1
    q = q_ref[0, 0]

    def scores(j):
        lo, hi = bounds[j], bounds[j + 1]
        s_sc[j % 2, :, :hi - lo] = _dot(q, k_ref[0, 0, :, lo:hi])

    def softmax(j):
        slot = j % 2
        nc = (bounds[j + 1] - bounds[j]) // LANES
        for r in range(tq // rb):
            rows = slice(r * rb, (r + 1) * rb)
            cols = [s_sc[slot, rows, c * LANES:(c + 1) * LANES] for c in range(nc)]
            mx = cols[0]
            for c in range(1, nc):
                mx = jnp.maximum(mx, cols[c])
            m_cur = jnp.broadcast_to(jnp.max(mx, axis=-1, keepdims=True), (rb, LANES))
            if j == 0:
                m_new = m_cur
            else:
                m_prev = m_sc[rows, :]
                m_new = jnp.maximum(m_prev, m_cur)
                a_sc[slot, rows, :] = jnp.exp2(m_prev - m_new)
            m_sc[rows, :] = m_new
            for c in range(nc):
                p_sc[slot, rows, c * LANES:(c + 1) * LANES] = jnp.exp2((cols[c] - m_new).astype(BF16))

    def values(j, lagged):
        slot = j % 2
        lo, hi = bounds[j], bounds[j + 1]
        pv = _dot(p_sc[slot, :, :hi - lo], v_ref[0, 0, lo:hi, :])
        if j == 0:
            acc_sc[...] = pv
            return None
        alpha = a_sc[slot]
        alpha2 = jnp.concatenate([alpha, alpha], axis=-1)
        acc_sc[...] = alpha2 * (acc_sc[...] + pv) if lagged else alpha2 * acc_sc[...] + pv
        return jnp.min(alpha, axis=0, keepdims=True)

    def fused(j):
        slot = j % 2
        lo, hi = bounds[j], bounds[j + 1]
        nc = (hi - lo) // LANES
        half = max(tq // 2, rb)
        for r in range(tq // rb):
            rows = slice(r * rb, (r + 1) * rb)
            if (r * rb) % half == 0:
                s = _dot(q[r * rb:r * rb + half], k_ref[0, 0, :, lo:hi])
            off = (r * rb) % half
            m_stab = m_sc[rows, :]
            cols = [s[off:off + rb, c * LANES:(c + 1) * LANES] for c in range(nc)]
            mx = cols[0]
            for c in range(1, nc):
                mx = jnp.maximum(mx, cols[c])
            m_cur = jnp.broadcast_to(jnp.max(mx, axis=-1, keepdims=True), (rb, LANES))
            m_new = jnp.maximum(m_stab, m_cur)
            a_sc[slot, rows, :] = jnp.exp2(m_stab - m_new)
            m_sc[rows, :] = m_new
            for c in range(nc):
                p_sc[slot, rows, c * LANES:(c + 1) * LANES] = jnp.exp2((cols[c] - m_stab).astype(BF16))

    def finish():
        o_ref[0] = (acc_sc[:, :V_HEAD] / acc_sc[:, V_HEAD:]).astype(BF16)

    scores(0)
    softmax(0)
    amin = None
    for j in range(1, n + 1):
        a = values(j - 1, True)
        if a is not None:
            amin = a if amin is None else jnp.minimum(amin, a)
        if j < n:
            fused(j)
    finish()
    if amin is not None:
        @pl.when(jnp.min(amin) < 2.0 ** -MAX_JUMP)
        def _():
            scores(0)
            for j in range(n):
                if j + 1 < n:
                    scores(j + 1)
                if j >= 1:
                    values(j - 1, False)
                softmax(j)
            values(n - 1, False)
            finish()


def _attention(q, k, v, tq, tk):
    B, H, S, _ = q.shape
    return pl.pallas_call(
        functools.partial(_attn_kernel, tk=tk, rb=min(32, tq)),
        grid=(B, H, S // tq),
        in_specs=[
            pl.BlockSpec((1, 1, tq, QK_PAD), lambda b, h, i: (b, h, i, 0)),
            pl.BlockSpec((1, 1, QK_PAD, S), lambda b, h, i: (b, h, 0, 0)),
            pl.BlockSpec((1, 1, S, 2 * V_HEAD), lambda b, h, i: (b, h, 0, 0)),
        ],
        out_specs=pl.BlockSpec((1, tq, V_HEAD), lambda b, h, i: (b, i, h)),
        out_shape=jax.ShapeDtypeStruct((B, S, H * V_HEAD), BF16),
        scratch_shapes=[
            pltpu.VMEM((2, tq, tk), F32),
            pltpu.VMEM((2, tq, tk), BF16),
            pltpu.VMEM((2, tq, LANES), F32),
            pltpu.VMEM((tq, LANES), F32),
            pltpu.VMEM((tq, 2 * V_HEAD), F32),
        ],
        compiler_params=_params("parallel", "parallel", "arbitrary"),
        name="attn",
    )(q, k, v)


def _seqdft_kernel(gc_ref, gs_ref, xr_ref, cd_ref, y_ref):
    dr, di = [], []
    for r in range(DFT_RADIX):
        x = xr_ref[r]
        dr.append(_dot(gc_ref[r], x))
        di.append(_dot(gs_ref[r], x))
    ar, br = dr[0] + dr[2], dr[0] - dr[2]
    ai, bi = di[0] + di[2], di[0] - di[2]
    cr, er = dr[1] + dr[3], dr[1] - dr[3]
    ci, ei = di[1] + di[3], di[1] - di[3]
    xre = (ar + cr, br + ei, ar - cr, br - ei)
    xim = (ai + ci, bi - er, ai - ci, bi + er)
    cd = cd_ref[...]
    for k2 in range(DFT_RADIX):
        re = xre[k2].astype(BF16)
        im = xim[k2].astype(BF16)
        for g in range(FOURIER_GROUPS):
            lo = g * FOURIER_GROUP_DIM
            z = jnp.concatenate([re[:, lo:lo + FOURIER_GROUP_DIM], im[:, lo:lo + FOURIER_GROUP_DIM]], axis=-1)
            y_ref[k2, :, lo:lo + FOURIER_GROUP_DIM] = _dot(z, cd).astype(BF16)


def _seqdft(gc, gs, xr, cd, tmr):
    R, M, N = xr.shape
    tn = D_FOURIER
    return pl.pallas_call(
        _seqdft_kernel,
        grid=(N // tn, M // tmr),
        in_specs=[
            pl.BlockSpec((R, tmr, M), lambda j, i: (0, i, 0)),
            pl.BlockSpec((R, tmr, M), lambda j, i: (0, i, 0)),
            pl.BlockSpec((R, M, tn), lambda j, i: (0, 0, j)),
            _const_spec(cd.shape),
        ],
        out_specs=pl.BlockSpec((R, tmr, tn), lambda j, i: (0, i, j)),
        out_shape=jax.ShapeDtypeStruct((R, M, N), BF16),
        compiler_params=_params("parallel", "parallel"),
        name="seqdft",
    )(gc, gs, xr, cd)


def _memkv_kernel(m_ref, g_ref, w_ref, mk_ref, mv_ref):
    h = _rms(m_ref[0], g_ref[...]).astype(BF16)
    kv = _dot(h, w_ref[...])
    mk_ref[0] = kv[:, :D_MEM].astype(BF16)
    mv_ref[0] = kv[:, D_MEM:].astype(BF16)


def _memkv(mem, g, w):
    B, M, D = mem.shape
    return pl.pallas_call(
        _memkv_kernel,
        grid=(B,),
        in_specs=[
            pl.BlockSpec((1, M, D), lambda b: (b, 0, 0)),
            _const_spec((1, D)),
            _const_spec((D, 2 * D_MEM)),
        ],
        out_specs=[
            pl.BlockSpec((1, M, D_MEM), lambda b: (b, 0, 0)),
            pl.BlockSpec((1, M, D_MEM), lambda b: (b, 0, 0)),
        ],
        out_shape=[
            jax.ShapeDtypeStruct((B, M, D_MEM), BF16),
            jax.ShapeDtypeStruct((B, M, D_MEM), BF16),
        ],
        compiler_params=_params("parallel"),
        name="memkv",
    )(mem, g, w)


def _merge_kernel(x_ref, g_ref, wg_ref, wqm_ref, att_ref, y_ref, mk_ref, mv_ref,
                  womla_ref, wf_ref, womem_ref, wout_ref, o_ref):
    x = x_ref[0]
    D = x.shape[-1]
    h = _rms(x, g_ref[...]).astype(BF16)
    qm = _dot(h, wqm_ref[...]).astype(BF16)
    mk = mk_ref[0]
    mv = mv_ref[0]
    mscale = MEM_HEAD_DIM ** -0.5
    ctx = []
    for hd in range(MEM_HEADS):
        lo = hd * MEM_HEAD_DIM
        s = _dot_nt(qm[:, lo:lo + MEM_HEAD_DIM], mk[:, lo:lo + MEM_HEAD_DIM]) * mscale
        s = s - jnp.max(s, axis=-1, keepdims=True)
        e = jnp.exp(s)
        pm = (e / jnp.sum(e, axis=-1, keepdims=True)).astype(BF16)
        ctx.append(_dot(pm, mv[:, lo:lo + MEM_HEAD_DIM]).astype(BF16))
    c = _dot(jnp.concatenate(ctx, axis=-1), womem_ref[...])
    a = _dot(att_ref[0], womla_ref[...])
    f = _dot(y_ref[...], wf_ref[...])
    merged = jax.nn.sigmoid(_dot(h, wg_ref[:, :D])) * a
    merged += jax.nn.sigmoid(_dot(h, wg_ref[:, D:2 * D])) * f
    merged += jax.nn.sigmoid(_dot(h, wg_ref[:, 2 * D:])) * c
    o_ref[0] = x + _dot(merged.astype(BF16), wout_ref[...])


def _merge(x, g, wg, wqm, att, y, mk, mv, womla, wf, womem, wout, tm):
    B, S, D = x.shape
    M = mk.shape[1]
    return pl.pallas_call(
        _merge_kernel,
        grid=(B, S // tm),
        in_specs=[
            pl.BlockSpec((1, tm, D), lambda b, i: (b, i, 0)),
            _const_spec((1, D)),
            _const_spec(wg.shape),
            _const_spec(wqm.shape),
            pl.BlockSpec((1, tm, MLA_HEADS * V_HEAD), lambda b, i: (b, i, 0)),
            pl.BlockSpec((tm, D_FOURIER), lambda b, i: (i, b)),
            pl.BlockSpec((1, M, D_MEM), lambda b, i: (b, 0, 0)),
            pl.BlockSpec((1, M, D_MEM), lambda b, i: (b, 0, 0)),
            _const_spec(womla.shape),
            _const_spec(wf.shape),
            _const_spec(womem.shape),
            _const_spec(wout.shape),
        ],
        out_specs=pl.BlockSpec((1, tm, D), lambda b, i: (b, i, 0)),
        out_shape=jax.ShapeDtypeStruct((B, S, D), F32),
        compiler_params=_params("parallel", "parallel"),
        name="merge",
    )(x, g, wg, wqm, att, y, mk, mv, womla, wf, womem, wout)


def _ffn_kernel(x_ref, g_ref, wgu_ref, wd_ref, fg_ref, o_ref, *, final_norm):
    x = x_ref[...]
    d_ff = wd_ref.shape[0]
    h = _rms(x, g_ref[...]).astype(BF16)
    gate = _dot(h, wgu_ref[:, :d_ff])
    up = _dot(h, wgu_ref[:, d_ff:])
    act = (gate * jax.nn.sigmoid(gate) * up).astype(BF16)
    acc = x + _dot(act, wd_ref[...])
    if final_norm:
        acc = _rms(acc, fg_ref[...])
    o_ref[...] = acc


def _ffn(x, g, wgu, wd, fg, tm, final_norm):
    T, D = x.shape
    return pl.pallas_call(
        functools.partial(_ffn_kernel, final_norm=final_norm),
        grid=(T // tm,),
        in_specs=[
            pl.BlockSpec((tm, D), lambda i: (i, 0)),
            _const_spec((1, D)),
            _const_spec(wgu.shape),
            _const_spec(wd.shape),
            _const_spec((1, D)),
        ],
        out_specs=pl.BlockSpec((tm, D), lambda i: (i, 0)),
        out_shape=jax.ShapeDtypeStruct((T, D), F32),
        compiler_params=_params("parallel"),
        name="ffn_final" if final_norm else "ffn",
    )(x, g, wgu, wd, fg)


def _rope_tables(S):
    inv_freq = ROPE_THETA ** (-(jnp.arange(0, QK_ROPE, 2, dtype=F32) / QK_ROPE))
    ang = jnp.arange(S, dtype=F32)[:, None] * inv_freq[None, :]
    c, s, z = jnp.cos(ang), jnp.sin(ang), jnp.zeros_like(ang)
    return jnp.concatenate([c, z, c, z], axis=-1), jnp.concatenate([-s, z, s, z], axis=-1)


def _channel_dft_table():
    n = FOURIER_GROUP_DIM
    idx = jnp.arange(n, dtype=jnp.int32)
    ang = ((idx[:, None] * idx[None, :]) % n).astype(F32) * (2.0 * np.pi / n)
    return (jnp.concatenate([jnp.cos(ang), jnp.sin(ang)], axis=0) * n ** -0.5).astype(BF16)


def _seq_dft_tables(S):
    R = DFT_RADIX
    M = S // R
    n = R * jnp.arange(M, dtype=jnp.int32)[None, None, :] + jnp.arange(R, dtype=jnp.int32)[:, None, None]

    def exact(k):
        ang = ((k[None, :, None] * n) % S).astype(F32) * (2.0 * np.pi / S)
        return jnp.cos(ang), jnp.sin(ang)

    hc, hs = exact(SUBLANES * jnp.arange(M // SUBLANES, dtype=jnp.int32))
    lc, ls = exact(jnp.arange(SUBLANES, dtype=jnp.int32))
    norm = S ** -0.5
    lc, ls = lc * norm, ls * norm
    c = hc[:, :, None, :] * lc[:, None, :, :] - hs[:, :, None, :] * ls[:, None, :, :]
    s = hs[:, :, None, :] * lc[:, None, :, :] + hc[:, :, None, :] * ls[:, None, :, :]
    return c.reshape(R, M, M).astype(BF16), (-s).reshape(R, M, M).astype(BF16)


def _pad_rope_cols(w):
    half = QK_ROPE // 2
    z = jnp.zeros(w.shape[:-1] + (half,), w.dtype)
    return jnp.concatenate([w[..., :half], z, w[..., half:], z], axis=-1)


def _layer_weights(l, attn_norm, w_in, q_norm, w_uq, kv_norm, w_ukv, w_o_mla, w_fourier, mem_norm,
                   w_mem_kv, w_o_mem, w_out, ffn_norm, w_gate_up, w_down):
    o1 = Q_LORA
    o2 = o1 + KV_LORA
    o3 = o2 + QK_ROPE
    o4 = o3 + D_FOURIER
    o5 = o4 + D_MEM
    wi = w_in[l]
    w1 = jnp.concatenate([wi[:, :o2], _pad_rope_cols(wi[:, o2:o3]), wi[:, o3:o4]], axis=-1).astype(BF16)
    uq = w_uq[l].reshape(Q_LORA, MLA_HEADS, QK_NOPE + QK_ROPE)
    uq = jnp.concatenate([uq[..., :QK_NOPE], _pad_rope_cols(uq[..., QK_NOPE:])], axis=-1)
    uq = uq.reshape(Q_LORA, MLA_HEADS * QK_PAD).astype(BF16)
    return dict(
        attn_norm=attn_norm[l][None, :], w1=w1, q_norm=q_norm[l][None, :], wuq=uq,
        kv_norm=kv_norm[l][None, :], wukv=w_ukv[l].astype(BF16),
        wqm=wi[:, o4:o5].astype(BF16), wg=wi[:, o5:].astype(BF16),
        womla=w_o_mla[l].astype(BF16), wf=w_fourier[l].astype(BF16),
        mem_norm=mem_norm[l][None, :], wmemkv=w_mem_kv[l].astype(BF16), womem=w_o_mem[l].astype(BF16),
        wout=w_out[l].astype(BF16), ffn_norm=ffn_norm[l][None, :],
        wgu=w_gate_up[l].astype(BF16), wd=w_down[l].astype(BF16),
    )


def _tiles(S):
    return dict(tm=min(512, S), tq=min(1024, S), tk=min(1024, S))


def _group_layer(x, mem, w, tabs, final_gain, final_norm):
    B, S, D = x.shape
    t = _tiles(S)
    q, k, v, xr = _proj(x, w["attn_norm"], w["w1"], w["q_norm"], w["wuq"], w["kv_norm"], w["wukv"],
                        tabs["rc"], tabs["rs"], t["tm"])
    att = _attention(q, k, v, t["tq"], t["tk"])
    m = S // DFT_RADIX
    y = _seqdft(tabs["gc"], tabs["gs"], xr, tabs["cdft"], min(512 if m <= 1024 else 256, m))
    y = y.reshape(S, B * D_FOURIER)
    mk, mv = _memkv(mem, w["mem_norm"], w["wmemkv"])
    x1 = _merge(x, w["attn_norm"], w["wg"], w["wqm"], att, y, mk, mv,
                w["womla"], w["wf"], w["womem"], w["wout"], t["tm"])
    x2 = _ffn(x1.reshape(B * S, D), w["ffn_norm"], w["wgu"], w["wd"], final_gain, t["tm"], final_norm)
    return x2.reshape(B, S, D)


def _group_tables(S):
    rc, rs = _rope_tables(S)
    gc, gs = _seq_dft_tables(S)
    return dict(rc=rc, rs=rs, cdft=_channel_dft_table(), gc=gc, gs=gs)


def kernel(x_prompt, x_sample, mem_prompt, mem_sample, attn_norm, w_in, q_norm, w_uq, kv_norm, w_ukv,
           w_o_mla, w_fourier, mem_norm, w_mem_kv, w_o_mem, w_out, ffn_norm, w_gate_up, w_down, final_norm):
    depth = w_in.shape[0]
    tabs_p = _group_tables(x_prompt.shape[1])
    tabs_s = _group_tables(x_sample.shape[1])
    fg = final_norm[None, :]
    yp, ys = x_prompt, x_sample
    for l in range(depth):
        w = _layer_weights(l, attn_norm, w_in, q_norm, w_uq, kv_norm, w_ukv, w_o_mla, w_fourier, mem_norm,
                           w_mem_kv, w_o_mem, w_out, ffn_norm, w_gate_up, w_down)
        last = l == depth - 1
        yp = _group_layer(yp, mem_prompt, w, tabs_p, fg, last)
        ys = _group_layer(ys, mem_sample, w, tabs_s, fg, last)
    return (yp, ys)
```

```python
import functools

import jax
import jax.numpy as jnp
import numpy as np
from jax import lax
from jax.experimental import pallas as pl
from jax.experimental.pallas import tpu as pltpu

MLA_HEADS = 8
QK_NOPE = 128
QK_ROPE = 64
V_HEAD = 128
Q_LORA = 384
KV_LORA = 256
ROPE_THETA = 10000.0
FOURIER_GROUPS = 4
FOURIER_GROUP_DIM = 128
D_FOURIER = FOURIER_GROUPS * FOURIER_GROUP_DIM
MEM_HEADS = 4
MEM_HEAD_DIM = 128
D_MEM = MEM_HEADS * MEM_HEAD_DIM
EPS = 1e-6

LANES = 128
QK_PAD = QK_NOPE + LANES
VMEM_LIMIT = 56 * 1024 * 1024
DFT_RADIX = 4
SUBLANES = 8
MAX_JUMP = 8.0
Q_PRESCALE = float((QK_NOPE + QK_ROPE) ** -0.5 * np.log2(np.e))

F32 = jnp.float32
BF16 = jnp.bfloat16


def _rms(x, g):
    return x * lax.rsqrt(jnp.mean(x * x, axis=-1, keepdims=True) + EPS) * g


def _dot(a, b):
    return jnp.dot(a, b, preferred_element_type=F32)


def _dot_nt(a, b):
    return lax.dot_general(a, b, (((1,), (1,)), ((), ())), preferred_element_type=F32)


def _rope(x, c, s):
    return x * c + pltpu.roll(x, LANES // 2, axis=1) * s


def _const_spec(shape):
    nd = len(shape)
    return pl.BlockSpec(shape, lambda *_: (0,) * nd, pipeline_mode=pl.Buffered(1))


def _params(*sem, flags=None):
    return pltpu.CompilerParams(dimension_semantics=sem, vmem_limit_bytes=VMEM_LIMIT, flags=flags)


def _proj_kernel(x_ref, g_ref, w1_ref, qn_ref, wuq_ref, kvn_ref, wukv_ref, rc_ref, rs_ref,
                 q_ref, k_ref, v_ref, xr_ref, f_sc):
    x = x_ref[0]
    tm = x.shape[0]
    h = _rms(x, g_ref[...]).astype(BF16)
    p = _dot(h, w1_ref[...])
    o1 = Q_LORA
    o2 = o1 + KV_LORA
    o3 = o2 + LANES
    rc = rc_ref[...]
    rs = rs_ref[...]
    cq = _rms(p[:, :o1], qn_ref[...]).astype(BF16)
    q = _dot(cq, wuq_ref[...]) * Q_PRESCALE
    ckv = _rms(p[:, o1:o2], kvn_ref[...]).astype(BF16)
    kv = _dot(ckv, wukv_ref[...])
    kpe_t = _rope(p[:, o2:o3], rc, rs).T.astype(BF16)
    for hd in range(MLA_HEADS):
        b0 = hd * QK_PAD
        q_ref[0, hd, :, :QK_NOPE] = q[:, b0:b0 + QK_NOPE].astype(BF16)
        q_ref[0, hd, :, QK_NOPE:] = _rope(q[:, b0 + QK_NOPE:b0 + QK_PAD], rc, rs).astype(BF16)
        c0 = hd * (QK_NOPE + V_HEAD)
        k_ref[0, hd, :QK_NOPE, :] = kv[:, c0:c0 + QK_NOPE].T.astype(BF16)
        k_ref[0, hd, QK_NOPE:, :] = kpe_t
        v_ref[0, hd, :, :V_HEAD] = kv[:, c0 + QK_NOPE:c0 + QK_NOPE + V_HEAD].astype(BF16)
        v_ref[0, hd, :, V_HEAD:] = jnp.ones((tm, V_HEAD), BF16)
    for g in range(FOURIER_GROUPS):
        lo = o3 + g * FOURIER_GROUP_DIM
        f_sc[g] = p[:, lo:lo + FOURIER_GROUP_DIM]
    for r in range(DFT_RADIX):
        for g in range(FOURIER_GROUPS):
            lo = g * FOURIER_GROUP_DIM
            xr_ref[r, :, lo:lo + FOURIER_GROUP_DIM] = f_sc[g, pl.ds(r, tm // DFT_RADIX, stride=DFT_RADIX), :].astype(BF16)


def _proj(x, g, w1, qn, wuq, kvn, wukv, rc, rs, tm):
    B, S, D = x.shape
    H = MLA_HEADS
    nw1 = w1.shape[1]
    grid = (B, S // tm)
    R = DFT_RADIX
    return pl.pallas_call(
        _proj_kernel,
        grid=grid,
        in_specs=[
            pl.BlockSpec((1, tm, D), lambda b, i: (b, i, 0)),
            _const_spec((1, D)),
            _const_spec((D, nw1)),
            _const_spec((1, Q_LORA)),
            _const_spec((Q_LORA, H * QK_PAD)),
            _const_spec((1, KV_LORA)),
            _const_spec((KV_LORA, H * (QK_NOPE + V_HEAD))),
            pl.BlockSpec((tm, LANES), lambda b, i: (i, 0)),
            pl.BlockSpec((tm, LANES), lambda b, i: (i, 0)),
        ],
        out_specs=[
            pl.BlockSpec((1, H, tm, QK_PAD), lambda b, i: (b, 0, i, 0)),
            pl.BlockSpec((1, H, QK_PAD, tm), lambda b, i: (b, 0, 0, i)),
            pl.BlockSpec((1, H, tm, 2 * V_HEAD), lambda b, i: (b, 0, i, 0)),
            pl.BlockSpec((R, tm // R, D_FOURIER), lambda b, i: (0, i, b)),
        ],
        out_shape=[
            jax.ShapeDtypeStruct((B, H, S, QK_PAD), BF16),
            jax.ShapeDtypeStruct((B, H, QK_PAD, S), BF16),
            jax.ShapeDtypeStruct((B, H, S, 2 * V_HEAD), BF16),
            jax.ShapeDtypeStruct((R, S // R, B * D_FOURIER), BF16),
        ],
        scratch_shapes=[pltpu.VMEM((FOURIER_GROUPS, tm, FOURIER_GROUP_DIM), F32)],
        compiler_params=_params("parallel", "parallel"),
        name="proj",
    )(x, g, w1, qn, wuq, kvn, wukv, rc, rs)


def _attn_kernel(q_ref, k_ref, v_ref, o_ref, s_sc, p_sc, a_sc, m_sc, acc_sc, *, tk, rb):
    S = k_ref.shape[3]
    tq = q_ref.shape[2]
    edge = tk // 2 if S >= 2 * tk else tk
    bounds = [0] + list(range(edge, S - edge + 1, tk)) + [S]
    n = len(bounds) - 1
    q = q_ref[0, 0]

    def scores(j):
        lo, hi = bounds[j], bounds[j + 1]
        s_sc[j % 2, :, :hi - lo] = _dot(q, k_ref[0, 0, :, lo:hi])

    def softmax(j):
        slot = j % 2
        nc = (bounds[j + 1] - bounds[j]) // LANES
        for r in range(tq // rb):
            rows = slice(r * rb, (r + 1) * rb)
            cols = [s_sc[slot, rows, c * LANES:(c + 1) * LANES] for c in range(nc)]
            mx = cols[0]
            for c in range(1, nc):
                mx = jnp.maximum(mx, cols[c])
            m_cur = jnp.broadcast_to(jnp.max(mx, axis=-1, keepdims=True), (rb, LANES))
            if j == 0:
                m_new = m_cur
            else:
                m_prev = m_sc[rows, :]
                m_new = jnp.maximum(m_prev, m_cur)
                a_sc[slot, rows, :] = jnp.exp2(m_prev - m_new)
            m_sc[rows, :] = m_new
            for c in range(nc):
                p_sc[slot, rows, c * LANES:(c + 1) * LANES] = jnp.exp2((cols[c] - m_new).astype(BF16))

    def values(j, lagged):
        slot = j % 2
        lo, hi = bounds[j], bounds[j + 1]
        pv = _dot(p_sc[slot, :, :hi - lo], v_ref[0, 0, lo:hi, :])
        if j == 0:
            acc_sc[...] = pv
            return None
        alpha = a_sc[slot]
        alpha2 = jnp.concatenate([alpha, alpha], axis=-1)
        acc_sc[...] = alpha2 * (acc_sc[...] + pv) if lagged else alpha2 * acc_sc[...] + pv
        return jnp.min(alpha, axis=0, keepdims=True)

    def fused(j):
        slot = j % 2
        lo, hi = bounds[j], bounds[j + 1]
        nc = (hi - lo) // LANES
        half = max(tq // 4, rb)
        for r in range(tq // rb):
            rows = slice(r * rb, (r + 1) * rb)
            if (r * rb) % half == 0:
                s = _dot(q[r * rb:r * rb + half], k_ref[0, 0, :, lo:hi])
            off = (r * rb) % half
            m_stab = m_sc[rows, :]
            cols = [s[off:off + rb, c * LANES:(c + 1) * LANES] for c in range(nc)]
            mx = cols[0]
            for c in range(1, nc):
                mx = jnp.maximum(mx, cols[c])
            m_cur = jnp.broadcast_to(jnp.max(mx, axis=-1, keepdims=True), (rb, LANES))
            m_new = jnp.maximum(m_stab, m_cur)
            a_sc[slot, rows, :] = jnp.exp2(m_stab - m_new)
            m_sc[rows, :] = m_new
            for c in range(nc):
                p_sc[slot, rows, c * LANES:(c + 1) * LANES] = jnp.exp2((cols[c] - m_stab).astype(BF16))

    def finish():
        o_ref[0] = (acc_sc[:, :V_HEAD] / acc_sc[:, V_HEAD:]).astype(BF16)

    scores(0)
    softmax(0)
    amin = None
    for j in range(1, n + 1):
        a = values(j - 1, True)
        if a is not None:
            amin = a if amin is None else jnp.minimum(amin, a)
        if j < n:
            fused(j)
    finish()
    if amin is not None:
        @pl.when(jnp.min(amin) < 2.0 ** -MAX_JUMP)
        def _():
            scores(0)
            for j in range(n):
                if j + 1 < n:
                    scores(j + 1)
                if j >= 1:
                    values(j - 1, False)
                softmax(j)
            values(n - 1, False)
            finish()


def _attention(q, k, v, tq, tk):
    B, H, S, _ = q.shape
    return pl.pallas_call(
        functools.partial(_attn_kernel, tk=tk, rb=min(32, tq)),
        grid=(B, H, S // tq),
        in_specs=[
            pl.BlockSpec((1, 1, tq, QK_PAD), lambda b, h, i: (b, h, i, 0)),
            pl.BlockSpec((1, 1, QK_PAD, S), lambda b, h, i: (b, h, 0, 0)),
            pl.BlockSpec((1, 1, S, 2 * V_HEAD), lambda b, h, i: (b, h, 0, 0)),
        ],
        out_specs=pl.BlockSpec((1, tq, V_HEAD), lambda b, h, i: (b, i, h)),
        out_shape=jax.ShapeDtypeStruct((B, S, H * V_HEAD), BF16),
        scratch_shapes=[
            pltpu.VMEM((2, tq, tk), F32),
            pltpu.VMEM((2, tq, tk), BF16),
            pltpu.VMEM((2, tq, LANES), F32),
            pltpu.VMEM((tq, LANES), F32),
            pltpu.VMEM((tq, 2 * V_HEAD), F32),
        ],
        compiler_params=_params("parallel", "parallel", "arbitrary"),
        name="attn",
    )(q, k, v)


def _seqdft_kernel(gc_ref, gs_ref, xr_ref, cd_ref, y_ref):
    dr, di = [], []
    for r in range(DFT_RADIX):
        x = xr_ref[r]
        dr.append(_dot(gc_ref[r], x))
        di.append(_dot(gs_ref[r], x))
    ar, br = dr[0] + dr[2], dr[0] - dr[2]
    ai, bi = di[0] + di[2], di[0] - di[2]
    cr, er = dr[1] + dr[3], dr[1] - dr[3]
    ci, ei = di[1] + di[3], di[1] - di[3]
    xre = (ar + cr, br + ei, ar - cr, br - ei)
    xim = (ai + ci, bi - er, ai - ci, bi + er)
    cd = cd_ref[...]
    for k2 in range(DFT_RADIX):
        re = xre[k2].astype(BF16)
        im = xim[k2].astype(BF16)
        for g in range(FOURIER_GROUPS):
            lo = g * FOURIER_GROUP_DIM
            z = jnp.concatenate([re[:, lo:lo + FOURIER_GROUP_DIM], im[:, lo:lo + FOURIER_GROUP_DIM]], axis=-1)
            y_ref[k2, :, lo:lo + FOURIER_GROUP_DIM] = _dot(z, cd).astype(BF16)


def _seqdft(gc, gs, xr, cd, tmr):
    R, M, N = xr.shape
    tn = D_FOURIER
    return pl.pallas_call(
        _seqdft_kernel,
        grid=(N // tn, M // tmr),
        in_specs=[
            pl.BlockSpec((R, tmr, M), lambda j, i: (0, i, 0)),
            pl.BlockSpec((R, tmr, M), lambda j, i: (0, i, 0)),
            pl.BlockSpec((R, M, tn), lambda j, i: (0, 0, j)),
            _const_spec(cd.shape),
        ],
        out_specs=pl.BlockSpec((R, tmr, tn), lambda j, i: (0, i, j)),
        out_shape=jax.ShapeDtypeStruct((R, M, N), BF16),
        compiler_params=_params("parallel", "parallel"),
        name="seqdft",
    )(gc, gs, xr, cd)


def _memkv_kernel(m_ref, g_ref, w_ref, mk_ref, mv_ref):
    h = _rms(m_ref[0], g_ref[...]).astype(BF16)
    kv = _dot(h, w_ref[...])
    mk_ref[0] = kv[:, :D_MEM].astype(BF16)
    mv_ref[0] = kv[:, D_MEM:].astype(BF16)


def _memkv(mem, g, w):
    B, M, D = mem.shape
    return pl.pallas_call(
        _memkv_kernel,
        grid=(B,),
        in_specs=[
            pl.BlockSpec((1, M, D), lambda b: (b, 0, 0)),
            _const_spec((1, D)),
            _const_spec((D, 2 * D_MEM)),
        ],
        out_specs=[
            pl.BlockSpec((1, M, D_MEM), lambda b: (b, 0, 0)),
            pl.BlockSpec((1, M, D_MEM), lambda b: (b, 0, 0)),
        ],
        out_shape=[
            jax.ShapeDtypeStruct((B, M, D_MEM), BF16),
            jax.ShapeDtypeStruct((B, M, D_MEM), BF16),
        ],
        compiler_params=_params("parallel"),
        name="memkv",
    )(mem, g, w)


def _merge_kernel(x_ref, g_ref, wg_ref, wqm_ref, att_ref, y_ref, mk_ref, mv_ref,
                  womla_ref, wf_ref, womem_ref, wout_ref, o_ref):
    x = x_ref[0]
    D = x.shape[-1]
    h = _rms(x, g_ref[...]).astype(BF16)
    qm = _dot(h, wqm_ref[...]).astype(BF16)
    mk = mk_ref[0]
    mv = mv_ref[0]
    mscale = MEM_HEAD_DIM ** -0.5
    ctx = []
    for hd in range(MEM_HEADS):
        lo = hd * MEM_HEAD_DIM
        s = _dot_nt(qm[:, lo:lo + MEM_HEAD_DIM], mk[:, lo:lo + MEM_HEAD_DIM]) * mscale
        s = s - jnp.max(s, axis=-1, keepdims=True)
        e = jnp.exp(s)
        pm = (e / jnp.sum(e, axis=-1, keepdims=True)).astype(BF16)
        ctx.append(_dot(pm, mv[:, lo:lo + MEM_HEAD_DIM]).astype(BF16))
    c = _dot(jnp.concatenate(ctx, axis=-1), womem_ref[...])
    a = _dot(att_ref[0], womla_ref[...])
    f = _dot(y_ref[...], wf_ref[...])
    merged = jax.nn.sigmoid(_dot(h, wg_ref[:, :D])) * a
    merged += jax.nn.sigmoid(_dot(h, wg_ref[:, D:2 * D])) * f
    merged += jax.nn.sigmoid(_dot(h, wg_ref[:, 2 * D:])) * c
    o_ref[0] = x + _dot(merged.astype(BF16), wout_ref[...])


def _merge(x, g, wg, wqm, att, y, mk, mv, womla, wf, womem, wout, tm):
    B, S, D = x.shape
    M = mk.shape[1]
    return pl.pallas_call(
        _merge_kernel,
        grid=(B, S // tm),
        in_specs=[
            pl.BlockSpec((1, tm, D), lambda b, i: (b, i, 0)),
            _const_spec((1, D)),
            _const_spec(wg.shape),
            _const_spec(wqm.shape),
            pl.BlockSpec((1, tm, MLA_HEADS * V_HEAD), lambda b, i: (b, i, 0)),
            pl.BlockSpec((tm, D_FOURIER), lambda b, i: (i, b)),
            pl.BlockSpec((1, M, D_MEM), lambda b, i: (b, 0, 0)),
            pl.BlockSpec((1, M, D_MEM), lambda b, i: (b, 0, 0)),
            _const_spec(womla.shape),
            _const_spec(wf.shape),
            _const_spec(womem.shape),
            _const_spec(wout.shape),
        ],
        out_specs=pl.BlockSpec((1, tm, D), lambda b, i: (b, i, 0)),
        out_shape=jax.ShapeDtypeStruct((B, S, D), F32),
        compiler_params=_params("parallel", "parallel"),
        name="merge",
    )(x, g, wg, wqm, att, y, mk, mv, womla, wf, womem, wout)


def _ffn_kernel(x_ref, g_ref, wgu_ref, wd_ref, fg_ref, o_ref, *, final_norm):
    x = x_ref[...]
    d_ff = wd_ref.shape[0]
    h = _rms(x, g_ref[...]).astype(BF16)
    gate = _dot(h, wgu_ref[:, :d_ff])
    up = _dot(h, wgu_ref[:, d_ff:])
    act = (gate * jax.nn.sigmoid(gate) * up).astype(BF16)
    acc = x + _dot(act, wd_ref[...])
    if final_norm:
        acc = _rms(acc, fg_ref[...])
    o_ref[...] = acc


def _ffn(x, g, wgu, wd, fg, tm, final_norm):
    T, D = x.shape
    return pl.pallas_call(
        functools.partial(_ffn_kernel, final_norm=final_norm),
        grid=(T // tm,),
        in_specs=[
            pl.BlockSpec((tm, D), lambda i: (i, 0)),
            _const_spec((1, D)),
            _const_spec(wgu.shape),
            _const_spec(wd.shape),
            _const_spec((1, D)),
        ],
        out_specs=pl.BlockSpec((tm, D), lambda i: (i, 0)),
        out_shape=jax.ShapeDtypeStruct((T, D), F32),
        compiler_params=_params("parallel"),
        name="ffn_final" if final_norm else "ffn",
    )(x, g, wgu, wd, fg)


def _rope_tables(S):
    inv_freq = ROPE_THETA ** (-(jnp.arange(0, QK_ROPE, 2, dtype=F32) / QK_ROPE))
    ang = jnp.arange(S, dtype=F32)[:, None] * inv_freq[None, :]
    c, s, z = jnp.cos(ang), jnp.sin(ang), jnp.zeros_like(ang)
    return jnp.concatenate([c, z, c, z], axis=-1), jnp.concatenate([-s, z, s, z], axis=-1)


def _channel_dft_table():
    n = FOURIER_GROUP_DIM
    idx = jnp.arange(n, dtype=jnp.int32)
    ang = ((idx[:, None] * idx[None, :]) % n).astype(F32) * (2.0 * np.pi / n)
    return (jnp.concatenate([jnp.cos(ang), jnp.sin(ang)], axis=0) * n ** -0.5).astype(BF16)


def _seq_dft_tables(S):
    R = DFT_RADIX
    M = S // R
    n = R * jnp.arange(M, dtype=jnp.int32)[None, None, :] + jnp.arange(R, dtype=jnp.int32)[:, None, None]

    def exact(k):
        ang = ((k[None, :, None] * n) % S).astype(F32) * (2.0 * np.pi / S)
        return jnp.cos(ang), jnp.sin(ang)

    hc, hs = exact(SUBLANES * jnp.arange(M // SUBLANES, dtype=jnp.int32))
    lc, ls = exact(jnp.arange(SUBLANES, dtype=jnp.int32))
    norm = S ** -0.5
    lc, ls = lc * norm, ls * norm
    c = hc[:, :, None, :] * lc[:, None, :, :] - hs[:, :, None, :] * ls[:, None, :, :]
    s = hs[:, :, None, :] * lc[:, None, :, :] + hc[:, :, None, :] * ls[:, None, :, :]
    return c.reshape(R, M, M).astype(BF16), (-s).reshape(R, M, M).astype(BF16)


def _pad_rope_cols(w):
    half = QK_ROPE // 2
    z = jnp.zeros(w.shape[:-1] + (half,), w.dtype)
    return jnp.concatenate([w[..., :half], z, w[..., half:], z], axis=-1)


def _layer_weights(l, attn_norm, w_in, q_norm, w_uq, kv_norm, w_ukv, w_o_mla, w_fourier, mem_norm,
                   w_mem_kv, w_o_mem, w_out, ffn_norm, w_gate_up, w_down):
    o1 = Q_LORA
    o2 = o1 + KV_LORA
    o3 = o2 + QK_ROPE
    o4 = o3 + D_FOURIER
    o5 = o4 + D_MEM
    wi = w_in[l]
    w1 = jnp.concatenate([wi[:, :o2], _pad_rope_cols(wi[:, o2:o3]), wi[:, o3:o4]], axis=-1).astype(BF16)
    uq = w_uq[l].reshape(Q_LORA, MLA_HEADS, QK_NOPE + QK_ROPE)
    uq = jnp.concatenate([uq[..., :QK_NOPE], _pad_rope_cols(uq[..., QK_NOPE:])], axis=-1)
    uq = uq.reshape(Q_LORA, MLA_HEADS * QK_PAD).astype(BF16)
    return dict(
        attn_norm=attn_norm[l][None, :], w1=w1, q_norm=q_norm[l][None, :], wuq=uq,
        kv_norm=kv_norm[l][None, :], wukv=w_ukv[l].astype(BF16),
        wqm=wi[:, o4:o5].astype(BF16), wg=wi[:, o5:].astype(BF16),
        womla=w_o_mla[l].astype(BF16), wf=w_fourier[l].astype(BF16),
        mem_norm=mem_norm[l][None, :], wmemkv=w_mem_kv[l].astype(BF16), womem=w_o_mem[l].astype(BF16),
        wout=w_out[l].astype(BF16), ffn_norm=ffn_norm[l][None, :],
        wgu=w_gate_up[l].astype(BF16), wd=w_down[l].astype(BF16),
    )


def _tiles(S):
    return dict(tm=min(512, S), tq=min(1024, S), tk=min(1024, S))


def _group_layer(x, mem, w, tabs, final_gain, final_norm):
    B, S, D = x.shape
    t = _tiles(S)
    q, k, v, xr = _proj(x, w["attn_norm"], w["w1"], w["q_norm"], w["wuq"], w["kv_norm"], w["wukv"],
                        tabs["rc"], tabs["rs"], t["tm"])
    att = _attention(q, k, v, t["tq"], t["tk"])
    m = S // DFT_RADIX
    y = _seqdft(tabs["gc"], tabs["gs"], xr, tabs["cdft"], min(512 if m <= 1024 else 256, m))
    y = y.reshape(S, B * D_FOURIER)
    mk, mv = _memkv(mem, w["mem_norm"], w["wmemkv"])
    x1 = _merge(x, w["attn_norm"], w["wg"], w["wqm"], att, y, mk, mv,
                w["womla"], w["wf"], w["womem"], w["wout"], t["tm"])
    x2 = _ffn(x1.reshape(B * S, D), w["ffn_norm"], w["wgu"], w["wd"], final_gain, t["tm"], final_norm)
    return x2.reshape(B, S, D)


def _group_tables(S):
    rc, rs = _rope_tables(S)
    gc, gs = _seq_dft_tables(S)
    return dict(rc=rc, rs=rs, cdft=_channel_dft_table(), gc=gc, gs=gs)


def kernel(x_prompt, x_sample, mem_prompt, mem_sample, attn_norm, w_in, q_norm, w_uq, kv_norm, w_ukv,
           w_o_mla, w_fourier, mem_norm, w_mem_kv, w_o_mem, w_out, ffn_norm, w_gate_up, w_down, final_norm):
    depth = w_in.shape[0]
    tabs_p = _group_tables(x_prompt.shape[1])
    tabs_s = _group_tables(x_sample.shape[1])
    fg = final_norm[None, :]
    yp, ys = x_prompt, x_sample
    for l in range(depth):
        w = _layer_weights(l, attn_norm, w_in, q_norm, w_uq, kv_norm, w_ukv, w_o_mla, w_fourier, mem_norm,
                           w_mem_kv, w_o_mem, w_out, ffn_norm, w_gate_up, w_down)
        last = l == depth - 1
        yp = _group_layer(yp, mem_prompt, w, tabs_p, fg, last)
        ys = _group_layer(ys, mem_sample, w, tabs_s, fg, last)
    return (yp, ys)
```
